```python
import jax, jax.numpy as jnp
from jax import lax
import numpy as np

D_MODEL = 1024
BATCH = 4
SEQ = 8192
DEPTH = 2

CHUNK = 64
N_META = 16
HEAD_DIM = 64
Q_BLOCK = 128
POOL_WINDOWS = (2, 4, 8, 16)
N_POOL_GROUPS = 4
POOL_WIDTH = D_MODEL // 4
POOL_GROUP = POOL_WIDTH // N_POOL_GROUPS
FOX_HEADS = (3 * D_MODEL // 8) // HEAD_DIM
FOX_WIDTH = FOX_HEADS * HEAD_DIM
SB_HEADS = (3 * D_MODEL // 8) // HEAD_DIM
SB_WIDTH = SB_HEADS * HEAD_DIM
MIX_WIDTH = POOL_WIDTH + FOX_WIDTH + SB_WIDTH
IN_WIDTH = POOL_WIDTH + 3 * FOX_WIDTH + FOX_HEADS + 3 * SB_WIDTH
IN_SPLITS = (
    POOL_WIDTH,
    POOL_WIDTH + FOX_WIDTH,
    POOL_WIDTH + 2 * FOX_WIDTH,
    POOL_WIDTH + 3 * FOX_WIDTH,
    POOL_WIDTH + 3 * FOX_WIDTH + FOX_HEADS,
    POOL_WIDTH + 3 * FOX_WIDTH + FOX_HEADS + SB_WIDTH,
    POOL_WIDTH + 3 * FOX_WIDTH + FOX_HEADS + 2 * SB_WIDTH,
)
D_FF = ((8 * D_MODEL // 3 + 255) // 256) * 256
EPS = 1e-6

kernel_name = 'hymba_pool_fox_stickbreak_trunk'


def rmsnorm(x, g):
    x32 = x.astype(jnp.float32)
    y = x32 * lax.rsqrt(jnp.mean(x32 * x32, axis=-1, keepdims=True) + EPS)
    return (y * g.astype(jnp.float32)).astype(x.dtype)


def head_rmsnorm(y, gain, n_heads):
    b, l, _ = y.shape
    y32 = y.reshape(b, l, n_heads, HEAD_DIM).astype(jnp.float32)
    y32 = y32 * lax.rsqrt(jnp.mean(y32 * y32, axis=-1, keepdims=True) + EPS)
    return (y32.reshape(b, l, n_heads * HEAD_DIM) * gain.astype(jnp.float32)).astype(y.dtype)


def split_heads(t, n_heads):
    b, l, _ = t.shape
    return t.reshape(b, l, n_heads, HEAD_DIM).transpose(0, 2, 1, 3)


def query_blocks(t):
    l = t.shape[2]
    t = t.reshape(t.shape[:2] + (l // Q_BLOCK, Q_BLOCK) + t.shape[3:])
    return jnp.moveaxis(t, 2, 0)


def merge_blocks(o):
    nblk, b, h, q, d = o.shape
    o = jnp.moveaxis(o, 0, 2).reshape(b, h, nblk * q, d)
    return o.transpose(0, 2, 1, 3).reshape(b, nblk * q, h * d)


def multiscale_pool_mixer(u, pool_w, pool_scale):
    b, l, _ = u.shape
    ug = u.reshape(b, l, N_POOL_GROUPS, POOL_GROUP).astype(jnp.float32)
    cs = jnp.cumsum(ug, axis=1)
    cs0 = jnp.concatenate([jnp.zeros_like(cs[:, :1]), cs], axis=1)
    t1 = jnp.arange(1, l + 1)[:, None]
    win = jnp.array(POOL_WINDOWS, dtype=jnp.int32)[None, :]
    lo = jnp.maximum(t1 - win, 0)
    window_sum = cs - cs0[:, lo, jnp.arange(N_POOL_GROUPS)[None, :]]
    mean = window_sum / (t1 - lo).astype(jnp.float32)[None, :, :, None]
    d = (mean - ug).astype(u.dtype)
    mixed = jnp.einsum('blgc,gcd->blgd', d, pool_w)
    return mixed.reshape(b, l, POOL_WIDTH) * pool_scale


def forgetting_attention(q, k, v, log_f):
    c = jnp.cumsum(log_f.astype(jnp.float32), axis=1).transpose(0, 2, 1)
    l = q.shape[2]
    pos = jnp.arange(l)
    k32 = k.astype(jnp.float32)
    scale = HEAD_DIM ** -0.5

    def block(args):
        qb, cb, tq = args
        s = jnp.einsum('bhqd,bhkd->bhqk', qb.astype(jnp.float32), k32) * scale
        s = s + cb[..., None] - c[:, :, None, :]
        s = jnp.where(pos[None, :] <= tq[:, None], s, -jnp.inf)
        p = jax.nn.softmax(s, axis=-1)
        return jnp.einsum('bhqk,bhkd->bhqd', p.astype(v.dtype), v)

    o = lax.map(block, (query_blocks(q), query_blocks(c), pos.reshape(-1, Q_BLOCK)))
    return merge_blocks(o)


def stick_breaking_attention(q, k, v):
    l = q.shape[2]
    pos = jnp.arange(l)
    k32 = k.astype(jnp.float32)
    scale = HEAD_DIM ** -0.5

    def block(args):
        qb, tq = args
        z = jnp.einsum('bhqd,bhkd->bhqk', qb.astype(jnp.float32), k32) * scale
        strict = pos[None, :] < tq[:, None]
        log_beta = jax.nn.log_sigmoid(z)
        log_1m = jnp.where(strict, jax.nn.log_sigmoid(-z), 0.0)
        later = lax.cumsum(log_1m, axis=3, reverse=True) - log_1m
        a = jnp.where(strict, jnp.exp(log_beta + later), 0.0)
        return jnp.einsum('bhqk,bhkd->bhqd', a.astype(v.dtype), v)

    o = lax.map(block, (query_blocks(q), pos.reshape(-1, Q_BLOCK)))
    return merge_blocks(o)


def setup_inputs(seed: int = 0) -> dict:
    key = jax.random.key(seed)
    ks = jax.random.split(key, 15)
    f32 = jnp.float32
    n = jax.random.normal
    return {
        'x': n(ks[0], (BATCH, SEQ, D_MODEL), f32),
        'meta_tokens': n(ks[1], (N_META, D_MODEL), f32),
        'norm1': 1.0 + 0.05 * n(ks[2], (DEPTH, D_MODEL), f32),
        'w_in': n(ks[3], (DEPTH, D_MODEL, IN_WIDTH), f32) * D_MODEL ** -0.5,
        'forget_bias': jax.random.uniform(ks[4], (DEPTH, FOX_HEADS), f32, 1.0, 4.0),
        'pool_w': n(ks[5], (DEPTH, N_POOL_GROUPS, POOL_GROUP, POOL_GROUP), f32) * POOL_GROUP ** -0.5,
        'pool_scale': 1.0 + 0.1 * n(ks[6], (DEPTH, POOL_WIDTH), f32),
        'fox_out_gain': 1.0 + 0.05 * n(ks[7], (DEPTH, FOX_WIDTH), f32),
        'sb_out_gain': 1.0 + 0.05 * n(ks[8], (DEPTH, SB_WIDTH), f32),
        'w_out': n(ks[9], (DEPTH, MIX_WIDTH, D_MODEL), f32) * MIX_WIDTH ** -0.5,
        'norm2': 1.0 + 0.05 * n(ks[10], (DEPTH, D_MODEL), f32),
        'w_gate': n(ks[11], (DEPTH, D_MODEL, D_FF), f32) * D_MODEL ** -0.5,
        'w_up': n(ks[12], (DEPTH, D_MODEL, D_FF), f32) * D_MODEL ** -0.5,
        'w_down': n(ks[13], (DEPTH, D_FF, D_MODEL), f32) * D_FF ** -0.5,
        'final_norm': 1.0 + 0.05 * n(ks[14], (D_MODEL,), f32),
    }


def reference(x, meta_tokens, norm1, w_in, forget_bias, pool_w, pool_scale, fox_out_gain,
              sb_out_gain, w_out, norm2, w_gate, w_up, w_down, final_norm):
    b, s_len, _ = x.shape
    l = N_META + s_len
    l_pad = -(-l // Q_BLOCK) * Q_BLOCK
    meta = jnp.broadcast_to(meta_tokens[None].astype(x.dtype), (b, N_META, D_MODEL))
    h = jnp.concatenate([meta, x, jnp.zeros((b, l_pad - l, D_MODEL), x.dtype)], axis=1)

    for i in range(DEPTH):
        a = rmsnorm(h, norm1[i])
        proj = a @ w_in[i]
        u, qf, kf, vf, fl, qs, ks_, vs = jnp.split(proj, IN_SPLITS, axis=-1)
        y_pool = multiscale_pool_mixer(u, pool_w[i], pool_scale[i])
        log_f = jax.nn.log_sigmoid(fl.astype(jnp.float32) + forget_bias[i].astype(jnp.float32))
        y_fox = forgetting_attention(split_heads(qf, FOX_HEADS), split_heads(kf, FOX_HEADS),
                                     split_heads(vf, FOX_HEADS), log_f)
        y_fox = head_rmsnorm(y_fox, fox_out_gain[i], FOX_HEADS)
        y_sb = stick_breaking_attention(split_heads(qs, SB_HEADS), split_heads(ks_, SB_HEADS),
                                        split_heads(vs, SB_HEADS))
        y_sb = head_rmsnorm(y_sb, sb_out_gain[i], SB_HEADS)
        h = h + jnp.concatenate([y_pool, y_fox, y_sb], axis=-1) @ w_out[i]
        a = rmsnorm(h, norm2[i])
        h = h + (jax.nn.silu(a @ w_gate[i]) * (a @ w_up[i])) @ w_down[i]

    h = rmsnorm(h, final_norm)
    return h[:, N_META:N_META + s_len]
```

```python
import functools
import math

import jax
import jax.numpy as jnp
from jax import lax
from jax.experimental import pallas as pl
from jax.experimental.pallas import tpu as pltpu

HEAD_DIM = 64
N_META = 16
EPS = 1e-6
POOL_WINDOWS = (2, 4, 8, 16)
POOL_GROUP = 64
LANES = 128
HALO = 16
LOG2E = math.log2(math.e)
NEG_BIG = -1e30
SB_ZERO_EXP = 110.0
VMEM_LIMIT = 56 * 1024 * 1024
F32 = jnp.float32
BF16 = jnp.bfloat16

ATTN_TILE = 256
ROW_TILE = 384
FF_CHUNKS = 2


def _dot(a, b):
    return jnp.dot(a, b, preferred_element_type=F32)


def _dot_nt(a, b):
    return lax.dot_general(a, b, (((1,), (1,)), ((), ())), preferred_element_type=F32)


def _rms(x, gain):
    ms = jnp.mean(x * x, axis=-1, keepdims=True)
    return x * lax.rsqrt(ms + EPS) * gain


def _split3(x):
    hi = x.astype(BF16)
    r = x - hi.astype(F32)
    mid = r.astype(BF16)
    lo = (r - mid.astype(F32)).astype(BF16)
    return hi, mid, lo


def _in_proj_kernel(h_ref, g_ref, wu_ref, wf_ref, wl_ref, ws_ref, fb_ref,
                    u_ref, qf_ref, kf_ref, vf_ref, c_ref, qs_ref, ks_ref, vs_ref,
                    carry_ref, *, tm, width):
    i = pl.program_id(1)

    @pl.when(i == 0)
    def _():
        carry_ref[...] = jnp.zeros_like(carry_ref)

    a = _rms(h_ref[0], g_ref[...]).astype(BF16)
    u_ref[0] = _dot(a, wu_ref[...])

    scale = HEAD_DIM ** -0.5
    pf = _dot(a, wf_ref[...])
    qf_ref[0] = (pf[:, :width] * (scale * LOG2E)).astype(BF16)
    kf_ref[0] = pf[:, width:2 * width].astype(BF16)
    vf_ref[0] = pf[:, 2 * width:].astype(BF16)
    ps = _dot(a, ws_ref[...])
    qs_ref[0] = (ps[:, :width] * scale).astype(BF16)
    ks_ref[0] = ps[:, width:2 * width].astype(BF16)
    vs_ref[0] = ps[:, 2 * width:].astype(BF16)

    fl = _dot(a, wl_ref[...]) + fb_ref[...]
    log_f = (jnp.minimum(fl, 0.0) - jnp.log1p(jnp.exp(-jnp.abs(fl)))) * LOG2E
    log_f_t = log_f.T[:8]
    row = lax.broadcasted_iota(jnp.int32, (tm, tm), 0)
    col = lax.broadcasted_iota(jnp.int32, (tm, tm), 1)
    tri = (row <= col).astype(BF16)
    hi, mid, lo = _split3(log_f_t)
    c = _dot(hi, tri) + _dot(mid, tri) + _dot(lo, tri) + carry_ref[:, :1]
    c_ref[0] = c
    carry_ref[...] = jnp.broadcast_to(c[:, tm - 1:tm], carry_ref.shape)


def _in_proj(h, g, wu, wf, wl, ws, fb, *, tm):
    b, lp, d = h.shape
    width = wf.shape[1] // 3
    nt = lp // tm
    row3 = lambda w: pl.BlockSpec((1, tm, w), lambda bi, i: (bi, i, 0))
    full = lambda arr: pl.BlockSpec(arr.shape, lambda bi, i: (0,) * arr.ndim)
    act = jax.ShapeDtypeStruct((b, lp, width), BF16)
    return pl.pallas_call(
        functools.partial(_in_proj_kernel, tm=tm, width=width),
        grid=(b, nt),
        in_specs=[row3(d), full(g), full(wu), full(wf), full(wl), full(ws), full(fb)],
        out_specs=[row3(wu.shape[1]), row3(width), row3(width), row3(width),
                   pl.BlockSpec((1, 8, tm), lambda bi, i: (bi, 0, i)),
                   row3(width), row3(width), row3(width)],
        out_shape=[jax.ShapeDtypeStruct((b, lp, wu.shape[1]), F32), act, act, act,
                   jax.ShapeDtypeStruct((b, 8, lp), F32), act, act, act],
        scratch_shapes=[pltpu.VMEM((8, LANES), F32)],
        compiler_params=pltpu.CompilerParams(
            dimension_semantics=("parallel", "arbitrary"), vmem_limit_bytes=VMEM_LIMIT),
        name="in_proj",
    )(h, g, wu, wf, wl, ws, fb)


def _pair_queries(q):
    first = lax.broadcasted_iota(jnp.int32, q.shape, 1) < HEAD_DIM
    zero = jnp.zeros_like(q)
    return (jnp.where(first, q, zero), jnp.where(first, zero, q)), first


def _pair_head_norm(o, first, gain):
    sq = o * o
    s_first = jnp.sum(jnp.where(first, sq, 0.0), axis=1, keepdims=True)
    s_second = jnp.sum(sq, axis=1, keepdims=True) - s_first
    inv = jnp.where(first, lax.rsqrt(s_first / HEAD_DIM + EPS), lax.rsqrt(s_second / HEAD_DIM + EPS))
    return o * inv * gain


def _attn_specs(b, lp, width, t):
    q_spec = pl.BlockSpec((1, t, LANES), lambda bi, p, qi: (bi, qi, p))
    kv_spec = pl.BlockSpec((1, lp, LANES), lambda bi, p, qi: (bi, 0, p))
    gain_spec = pl.BlockSpec((1, LANES), lambda bi, p, qi: (0, p))
    return q_spec, kv_spec, gain_spec


def _fox_kernel(q_ref, k_ref, v_ref, c_ref, g_ref, o_ref, *, t):
    qi = pl.program_id(2)
    q_heads, first = _pair_queries(q_ref[0])
    q0 = pl.multiple_of(qi * t, t)
    c_base = [c_ref[0, 0, hh:hh + 1, pl.ds(q0, LANES)][:, :1] for hh in (0, 1)]
    row = lax.broadcasted_iota(jnp.int32, (t, t), 0)
    col = lax.broadcasted_iota(jnp.int32, (t, t), 1)

    def tile(j, state, masked):
        k0 = pl.multiple_of(j * t, t)
        k = k_ref[0, pl.ds(k0, t), :]
        v = v_ref[0, pl.ds(k0, t), :]
        out = []
        for hh in (0, 1):
            m, l, acc = state[hh]
            bias = c_ref[0, 0, hh:hh + 1, pl.ds(k0, t)] - c_base[hh]
            s = _dot_nt(q_heads[hh], k) - bias
            if masked:
                s = jnp.where(col <= row, s, NEG_BIG)
            m_new = jnp.maximum(m, jnp.max(s, axis=1, keepdims=True))
            pr = jnp.exp2(s - m_new)
            alpha = jnp.exp2(m - m_new)
            l = alpha * l + jnp.sum(pr, axis=1, keepdims=True)
            acc = alpha * acc + _dot(pr.astype(BF16), v)
            out.append((m_new, l, acc))
        return tuple(out)

    init = tuple((jnp.full((t, 1), NEG_BIG, F32), jnp.zeros((t, 1), F32), jnp.zeros((t, LANES), F32))
                 for _ in (0, 1))
    state = lax.fori_loop(0, qi, lambda j, s: tile(j, s, False), init)
    (_, l0, acc0), (_, l1, acc1) = tile(qi, state, True)
    o = jnp.where(first, acc0 / l0, acc1 / l1)
    o_ref[0] = _pair_head_norm(o, first, g_ref[...]).astype(o_ref.dtype)


def _fox_attention(q, k, v, c, gain, *, t):
    b, lp, width = q.shape
    q_spec, kv_spec, gain_spec = _attn_specs(b, lp, width, t)
    c = c.reshape(b, c.shape[1] // 2, 2, lp)
    return pl.pallas_call(
        functools.partial(_fox_kernel, t=t),
        grid=(b, width // LANES, lp // t),
        in_specs=[q_spec, kv_spec, kv_spec,
                  pl.BlockSpec((1, 1, 2, lp), lambda bi, p, qi: (bi, p, 0, 0)), gain_spec],
        out_specs=q_spec,
        out_shape=jax.ShapeDtypeStruct((b, lp, width), BF16),
        compiler_params=pltpu.CompilerParams(
            dimension_semantics=("parallel", "parallel", "parallel"), vmem_limit_bytes=VMEM_LIMIT),
        name="fox_attention",
    )(q, k, v, c, gain)


def _sb_kernel(q_ref, k_ref, v_ref, tri_ref, g_ref, o_ref, *, t):
    qi = pl.program_id(2)
    q_heads, first = _pair_queries(q_ref[0])
    row = lax.broadcasted_iota(jnp.int32, (t, t), 0)
    col = lax.broadcasted_iota(jnp.int32, (t, t), 1)

    def tile(j, state, masked):
        k0 = pl.multiple_of(j * t, t)
        k = k_ref[0, pl.ds(k0, t), :]
        v = v_ref[0, pl.ds(k0, t), :]
        tri = tri_ref[...]
        out = []
        for hh in (0, 1):
            tail, acc = state[hh]
            z = _dot_nt(q_heads[hh], k)
            sp = jnp.maximum(z, 0.0) + jnp.log1p(jnp.exp(-jnp.abs(z)))
            if masked:
                sp = jnp.where(col < row, sp, 0.0)
            hi = sp.astype(BF16)
            lo = (sp - hi.astype(F32)).astype(BF16)
            cum = _dot(hi, tri) + _dot(lo, tri)
            e = z - cum + tail
            if masked:
                e = jnp.where(col < row, e, NEG_BIG)
            acc = acc + _dot(jnp.exp(e).astype(BF16), v)
            tail = tail - jnp.sum(sp, axis=1, keepdims=True)
            out.append((tail, acc))
        return tuple(out)

    init = tuple((jnp.zeros((t, 1), F32), jnp.zeros((t, LANES), F32)) for _ in (0, 1))
    state = tile(qi, init, True)

    def live(state):
        return jnp.max(jnp.maximum(state[0][0], state[1][0])) > -SB_ZERO_EXP

    def cond(carry):
        j, go, _ = carry
        return jnp.logical_and(j >= 0, go)

    def body(carry):
        j, _, state = carry
        state = tile(j, state, False)
        return j - 1, live(state), state

    _, _, ((_, acc0), (_, acc1)) = lax.while_loop(cond, body, (qi - 1, live(state), state))
    o = jnp.where(first, acc0, acc1)
    o_ref[0] = _pair_head_norm(o, first, g_ref[...]).astype(o_ref.dtype)


def _sb_attention(q, k, v, gain, *, t):
    b, lp, width = q.shape
    q_spec, kv_spec, gain_spec = _attn_specs(b, lp, width, t)
    idx = jnp.arange(t)
    tri = (idx[:, None] >= idx[None, :]).astype(BF16)
    return pl.pallas_call(
        functools.partial(_sb_kernel, t=t),
        grid=(b, width // LANES, lp // t),
        in_specs=[q_spec, kv_spec, kv_spec,
                  pl.BlockSpec((t, t), lambda bi, p, qi: (0, 0)), gain_spec],
        out_specs=q_spec,
        out_shape=jax.ShapeDtypeStruct((b, lp, width), BF16),
        compiler_params=pltpu.CompilerParams(
            dimension_semantics=("parallel", "parallel", "parallel"), vmem_limit_bytes=VMEM_LIMIT),
        name="sb_attention",
    )(q, k, v, tri, gain)


def _out_proj_kernel(h_ref, u_ref, up_ref, yf_ref, ys_ref, pw_ref, psc_ref,
                     wop_ref, wof_ref, wos_ref, o_ref, *, tm):
    i = pl.program_id(1)
    u = u_ref[0]
    halo = jnp.where(i == 0, 0.0, up_ref[0])
    x = jnp.concatenate([halo, u], axis=0)
    sums = []
    shift = 1
    for _ in POOL_WINDOWS:
        x = x + pltpu.roll(x, shift, 0)
        sums.append(x[HALO:])
        shift *= 2
    group = lax.broadcasted_iota(jnp.int32, u.shape, 1) // POOL_GROUP
    window_sum = sums[-1]
    window = jnp.full(u.shape, POOL_WINDOWS[-1], jnp.int32)
    for g in range(len(POOL_WINDOWS) - 2, -1, -1):
        window_sum = jnp.where(group == g, sums[g], window_sum)
        window = jnp.where(group == g, POOL_WINDOWS[g], window)
    t1 = i * tm + lax.broadcasted_iota(jnp.int32, u.shape, 0) + 1
    count = jnp.minimum(t1, window).astype(F32)
    d = (window_sum / count - u).astype(BF16)
    y_pool = (_dot(d, pw_ref[...]) * psc_ref[...]).astype(BF16)
    o_ref[0] = (h_ref[0] + _dot(y_pool, wop_ref[...]) + _dot(yf_ref[0], wof_ref[...])
                + _dot(ys_ref[0], wos_ref[...]))


def _out_proj(h, u, yf, ys, pw, psc, wop, wof, wos, *, tm):
    b, lp, d = h.shape
    row3 = lambda w: pl.BlockSpec((1, tm, w), lambda bi, i: (bi, i, 0))
    full = lambda arr: pl.BlockSpec(arr.shape, lambda bi, i: (0,) * arr.ndim)
    halo_spec = pl.BlockSpec((1, HALO, u.shape[2]),
                             lambda bi, i: (bi, jnp.maximum(i * (tm // HALO) - 1, 0), 0))
    return pl.pallas_call(
        functools.partial(_out_proj_kernel, tm=tm),
        grid=(b, lp // tm),
        in_specs=[row3(d), row3(u.shape[2]), halo_spec, row3(yf.shape[2]), row3(ys.shape[2]),
                  full(pw), full(psc), full(wop), full(wof), full(wos)],
        out_specs=row3(d),
        out_shape=jax.ShapeDtypeStruct((b, lp, d), F32),
        compiler_params=pltpu.CompilerParams(
            dimension_semantics=("parallel", "parallel"), vmem_limit_bytes=VMEM_LIMIT),
        name="out_proj",
    )(h, u, u, yf, ys, pw, psc, wop, wof, wos)


def _ffn_kernel(h_ref, g_ref, wg_ref, wu_ref, wd_ref, fg_ref, o_ref, *, chunks, final_norm):
    h = h_ref[...]
    a = _rms(h, g_ref[...]).astype(BF16)
    cw = wg_ref.shape[1] // chunks
    out = h
    for c in range(chunks):
        gate = _dot(a, wg_ref[:, c * cw:(c + 1) * cw])
        up = _dot(a, wu_ref[:, c * cw:(c + 1) * cw])
        act = (gate * jax.nn.sigmoid(gate) * up).astype(BF16)
        out = out + _dot(act, wd_ref[c * cw:(c + 1) * cw, :])
    if final_norm:
        out = _rms(out, fg_ref[...])
    o_ref[...] = out


def _ffn(h, g, wg, wu, wd, fg, *, tm, final_norm):
    n, d = h.shape
    row = pl.BlockSpec((tm, d), lambda i: (i, 0))
    resident = lambda arr: pl.BlockSpec(arr.shape, lambda i: (0,) * arr.ndim,
                                        pipeline_mode=pl.Buffered(1))
    return pl.pallas_call(
        functools.partial(_ffn_kernel, chunks=FF_CHUNKS, final_norm=final_norm),
        grid=(n // tm,),
        in_specs=[row, resident(g), resident(wg), resident(wu), resident(wd), resident(fg)],
        out_specs=row,
        out_shape=jax.ShapeDtypeStruct((n, d), F32),
        compiler_params=pltpu.CompilerParams(
            dimension_semantics=("parallel",), vmem_limit_bytes=VMEM_LIMIT),
        name="ffn",
    )(h, g, wg, wu, wd, fg)


def _block_diag(pool_w):
    groups, cin, cout = pool_w.shape
    out = jnp.zeros((groups * cin, groups * cout), pool_w.dtype)
    for g in range(groups):
        out = out.at[g * cin:(g + 1) * cin, g * cout:(g + 1) * cout].set(pool_w[g])
    return out


def _trunk(x, meta_tokens, norm1, w_in, forget_bias, pool_w, pool_scale, fox_out_gain,
           sb_out_gain, w_out, norm2, w_gate, w_up, w_down, final_norm, *, attn_tile, row_tile):
    b, s_len, d = x.shape
    depth = norm1.shape[0]
    l = N_META + s_len
    step = attn_tile * row_tile // math.gcd(attn_tile, row_tile)
    lp = -(-l // step) * step
    pool_width = pool_scale.shape[1]
    fox_w = fox_out_gain.shape[1]
    sb_w = sb_out_gain.shape[1]
    fox_heads = fox_w // HEAD_DIM
    meta = jnp.broadcast_to(meta_tokens[None].astype(x.dtype), (b, N_META, d))
    h = jnp.concatenate([meta, x, jnp.zeros((b, lp - l, d), x.dtype)], axis=1)

    o0 = pool_width
    o1 = o0 + 3 * fox_w
    o2 = o1 + fox_heads
    for i in range(depth):
        w = w_in[i].astype(BF16)
        w_forget = jnp.zeros((d, LANES), BF16).at[:, :fox_heads].set(w[:, o1:o2])
        fb = jnp.zeros((1, LANES), F32).at[0, :fox_heads].set(forget_bias[i].astype(F32))
        u, qf, kf, vf, c, qs, ks, vs = _in_proj(
            h, norm1[i][None], w[:, :o0], w[:, o0:o1], w_forget, w[:, o2:], fb, tm=row_tile)
        y_fox = _fox_attention(qf, kf, vf, c, fox_out_gain[i][None], t=attn_tile)
        y_sb = _sb_attention(qs, ks, vs, sb_out_gain[i][None], t=attn_tile)
        wo = w_out[i].astype(BF16)
        h = _out_proj(h, u, y_fox, y_sb, _block_diag(pool_w[i]).astype(BF16), pool_scale[i][None],
                      wo[:pool_width], wo[pool_width:pool_width + fox_w], wo[pool_width + fox_w:],
                      tm=row_tile)
        h = _ffn(h.reshape(b * lp, d), norm2[i][None], w_gate[i].astype(BF16), w_up[i].astype(BF16),
                 w_down[i].astype(BF16), final_norm[None], tm=row_tile,
                 final_norm=(i == depth - 1)).reshape(b, lp, d)
    return h[:, N_META:l]


def kernel(x, meta_tokens, norm1, w_in, forget_bias, pool_w, pool_scale, fox_out_gain, sb_out_gain,
           w_out, norm2, w_gate, w_up, w_down, final_norm):
    return _trunk(x, meta_tokens, norm1, w_in, forget_bias, pool_w, pool_scale, fox_out_gain,
                  sb_out_gain, w_out, norm2, w_gate, w_up, w_down, final_norm,
                  attn_tile=ATTN_TILE, row_tile=ROW_TILE)
```

```python
import functools
import math

import jax
import jax.numpy as jnp
from jax import lax
from jax.experimental import pallas as pl
from jax.experimental.pallas import tpu as pltpu

HEAD_DIM = 64
N_META = 16
EPS = 1e-6
POOL_WINDOWS = (2, 4, 8, 16)
POOL_GROUP = 64
LANES = 128
BF16_ROWS = 16
HALO = 16
PART_STRIDE = 8
LOG2E = math.log2(math.e)
NEG_BIG = -1e30
SB_ZERO_EXP = 160.0
VMEM_LIMIT = 56 * 1024 * 1024
F32 = jnp.float32
BF16 = jnp.bfloat16

ATTN_TILE = 256
FOX_Q_BLOCKS = 3
ROW_TILE = 384
FF_CHUNKS = 2


def _dot(a, b):
    return jnp.dot(a, b, preferred_element_type=F32)


def _rms(x, gain):
    ms = jnp.mean(x * x, axis=-1, keepdims=True)
    return x * lax.rsqrt(ms + EPS) * gain


def _split3(x):
    hi = x.astype(BF16)
    r = x - hi.astype(F32)
    mid = r.astype(BF16)
    lo = (r - mid.astype(F32)).astype(BF16)
    return hi, mid, lo


def _in_proj_kernel(h_ref, g_ref, wu_ref, wf_ref, wl_ref, ws_ref, fb_ref,
                    u_ref, qf_ref, kf_ref, vf_ref, cc_ref, qs_ref, ks_ref, vs_ref,
                    carry_ref, *, tm, width, n_heads):
    i = pl.program_id(1)

    @pl.when(i == 0)
    def _():
        carry_ref[...] = jnp.zeros_like(carry_ref)

    a = _rms(h_ref[0], g_ref[...]).astype(BF16)
    u_ref[0] = _dot(a, wu_ref[...])

    scale = HEAD_DIM ** -0.5
    pf = _dot(a, wf_ref[...])
    qf_ref[0] = (pf[:, :width] * (scale * LOG2E)).T.astype(BF16)
    kf_ref[0] = pf[:, width:2 * width].astype(BF16)
    vf_ref[0] = pf[:, 2 * width:].T.astype(BF16)
    ps = _dot(a, ws_ref[...])
    qs_ref[0] = (ps[:, :width] * (scale * LOG2E)).T.astype(BF16)
    ks_ref[0] = ps[:, width:2 * width].astype(BF16)
    vs_ref[0] = ps[:, 2 * width:].T.astype(BF16)

    fl = _dot(a, wl_ref[...]) + fb_ref[...]
    lane = lax.broadcasted_iota(jnp.int32, fl.shape, 1)
    log_f = jnp.where(lane < n_heads,
                      (jnp.minimum(fl, 0.0) - jnp.log1p(jnp.exp(-jnp.abs(fl)))) * LOG2E, 0.0)
    row = lax.broadcasted_iota(jnp.int32, (tm, tm), 0)
    col = lax.broadcasted_iota(jnp.int32, (tm, tm), 1)
    tri = (col <= row).astype(BF16)
    hi, mid, lo = _split3(log_f)
    c = _dot(tri, hi) + _dot(tri, mid) + _dot(tri, lo) + carry_ref[:1, :]
    carry_ref[...] = jnp.broadcast_to(c[tm - 1:tm, :], carry_ref.shape)
    hi, mid, lo = _split3(c)
    parts = (hi.astype(F32) + pltpu.roll(mid.astype(F32), PART_STRIDE, 1)
             + pltpu.roll(lo.astype(F32), 2 * PART_STRIDE, 1))
    cc_ref[0] = parts.astype(BF16)


def _in_proj(h, g, wu, wf, wl, ws, fb, *, tm, n_heads):
    b, lp, d = h.shape
    width = wf.shape[1] // 3
    row3 = lambda w: pl.BlockSpec((1, tm, w), lambda bi, i: (bi, i, 0))
    col3 = lambda w: pl.BlockSpec((1, w, tm), lambda bi, i: (bi, 0, i))
    full = lambda arr: pl.BlockSpec(arr.shape, lambda bi, i: (0,) * arr.ndim)
    rows = jax.ShapeDtypeStruct((b, lp, width), BF16)
    cols = jax.ShapeDtypeStruct((b, width, lp), BF16)
    return pl.pallas_call(
        functools.partial(_in_proj_kernel, tm=tm, width=width, n_heads=n_heads),
        grid=(b, lp // tm),
        in_specs=[row3(d), full(g), full(wu), full(wf), full(wl), full(ws), full(fb)],
        out_specs=[row3(wu.shape[1]), col3(width), row3(width), col3(width), row3(LANES),
                   col3(width), row3(width), col3(width)],
        out_shape=[jax.ShapeDtypeStruct((b, lp, wu.shape[1]), F32), cols, rows, cols,
                   jax.ShapeDtypeStruct((b, lp, LANES), BF16), cols, rows, cols],
        scratch_shapes=[pltpu.VMEM((8, LANES), F32)],
        compiler_params=pltpu.CompilerParams(
            dimension_semantics=("parallel", "arbitrary"), vmem_limit_bytes=VMEM_LIMIT),
        name="in_proj",
    )(h, g, wu, wf, wl, ws, fb)


def _head_norm_t(o_t):
    ms = jnp.mean(o_t * o_t, axis=0, keepdims=True)
    return o_t * lax.rsqrt(ms + EPS)


def _fox_kernel(qt_ref, k_ref, vt_ref, cc_ref, g_ref, o_ref, qa_ref, m_ref, acc_ref, *, tq, tk):
    p = pl.program_id(1)
    qi = pl.program_id(2)
    nb = tq // tk

    qt = qt_ref[0]
    r = lax.broadcasted_iota(jnp.int32, qt.shape, 0)
    zero = jnp.zeros_like(qt)
    for hh in (0, 1):
        own = (r < HEAD_DIM) if hh == 0 else (r >= HEAD_DIM)
        h = 2 * p + hh
        picks = (r == h) | (r == h + PART_STRIDE) | (r == h + 2 * PART_STRIDE)
        qa_ref[hh, :LANES, :] = jnp.where(own, qt, zero)
        qa_ref[hh, LANES:, :] = jnp.where(picks, -1.0, 0.0).astype(BF16)
    m_ref[...] = jnp.full(m_ref.shape, NEG_BIG, F32)
    acc_ref[...] = jnp.zeros(acc_ref.shape, F32)
    ones = jnp.ones((BF16_ROWS, tk), BF16)

    def key_tile(j, c0, masked):
        k0 = pl.multiple_of(j * tk, tk)
        kk = jnp.concatenate([k_ref[0, pl.ds(k0, tk), :], cc_ref[0, pl.ds(k0, tk), :]], axis=1)
        pieces = [(hh, c) for hh in (0, 1) for c in range(c0, tq, tk)]
        st = [_dot(kk, qa_ref[hh, :, c:c + tk]) for hh, c in pieces]
        if masked:
            row = lax.broadcasted_iota(jnp.int32, (tk, tk), 0)
            col = lax.broadcasted_iota(jnp.int32, (tk, tk), 1)
            st = [jnp.where(row <= col, s, NEG_BIG) if c == c0 else s for s, (_, c) in zip(st, pieces)]
        pt, alpha = [], []
        for s, (hh, c) in zip(st, pieces):
            m_old = m_ref[hh, :, c:c + tk]
            m_new = jnp.maximum(m_old, jnp.max(s, axis=0, keepdims=True))
            m_ref[hh, :, c:c + tk] = m_new
            alpha.append(jnp.exp2(m_old - m_new))
            pt.append(jnp.exp2(s - m_new).astype(BF16))
        for i, (hh, c) in enumerate(pieces):
            vt = jnp.concatenate(
                [vt_ref[0, hh * HEAD_DIM:(hh + 1) * HEAD_DIM, pl.ds(k0, tk)], ones], axis=0)
            acc_ref[hh, :, c:c + tk] = alpha[i] * acc_ref[hh, :, c:c + tk] + _dot(vt, pt[i])

    def full_tile(j, carry):
        key_tile(j, 0, False)
        return carry

    lax.fori_loop(0, nb * qi, full_tile, 0)
    for d in range(nb):
        key_tile(nb * qi + d, d * tk, True)

    out = []
    for hh in (0, 1):
        acc = acc_ref[hh]
        out.append(_head_norm_t(acc[:HEAD_DIM] / acc[HEAD_DIM:HEAD_DIM + 1]))
    o_ref[0] = (jnp.concatenate(out, axis=0).T * g_ref[...]).astype(o_ref.dtype)


def _fox_attention(qt, k, vt, cc, gain, *, tq, tk):
    b, lp, width = k.shape
    pairs = width // LANES
    return pl.pallas_call(
        functools.partial(_fox_kernel, tq=tq, tk=tk),
        grid=(b, pairs, lp // tq),
        in_specs=[pl.BlockSpec((1, LANES, tq), lambda bi, p, qi: (bi, p, qi)),
                  pl.BlockSpec((1, lp, LANES), lambda bi, p, qi: (bi, 0, p)),
                  pl.BlockSpec((1, LANES, lp), lambda bi, p, qi: (bi, p, 0)),
                  pl.BlockSpec((1, lp, LANES), lambda bi, p, qi: (bi, 0, 0)),
                  pl.BlockSpec((1, LANES), lambda bi, p, qi: (0, p))],
        out_specs=pl.BlockSpec((1, tq, LANES), lambda bi, p, qi: (bi, qi, p)),
        out_shape=jax.ShapeDtypeStruct((b, lp, width), BF16),
        scratch_shapes=[pltpu.VMEM((2, 2 * LANES, tq), BF16),
                        pltpu.VMEM((2, 1, tq), F32),
                        pltpu.VMEM((2, HEAD_DIM + BF16_ROWS, tq), F32)],
        compiler_params=pltpu.CompilerParams(
            dimension_semantics=("parallel", "parallel", "parallel"), vmem_limit_bytes=VMEM_LIMIT),
        name="fox_attention",
    )(qt, k, vt, cc, gain)


def _sb_kernel(qt_ref, k_ref, vt_ref, tri_ref, g_ref, o_ref, qa_ref, tail_ref, acc_ref, *, t, n_heads):
    qi = pl.program_id(1)
    for h in range(n_heads):
        qt = qt_ref[0, h * HEAD_DIM:(h + 1) * HEAD_DIM, :]
        zero = jnp.zeros_like(qt)
        qa_ref[h] = jnp.concatenate([qt, zero] if h % 2 == 0 else [zero, qt], axis=0)
    tail_ref[...] = jnp.zeros(tail_ref.shape, F32)
    acc_ref[...] = jnp.zeros(acc_ref.shape, F32)

    def key_tile(j, masked):
        k0 = pl.multiple_of(j * t, t)
        tri = tri_ref[...]
        heads = range(n_heads)
        if masked:
            row = lax.broadcasted_iota(jnp.int32, (t, t), 0)
            col = lax.broadcasted_iota(jnp.int32, (t, t), 1)
            strict = row < col
        z = [_dot(k_ref[0, pl.ds(k0, t), (h // 2) * LANES:(h // 2 + 1) * LANES], qa_ref[h])
             for h in heads]
        split = []
        for h in heads:
            sp = jnp.maximum(z[h], 0.0) + jnp.log(1.0 + jnp.exp2(-jnp.abs(z[h]))) * LOG2E
            if masked:
                sp = jnp.where(strict, sp, 0.0)
            hi = sp.astype(BF16)
            split.append((hi, (sp - hi.astype(F32)).astype(BF16)))
        cum = [_dot(tri, jnp.concatenate([hi, lo], axis=0)) for hi, lo in split]
        weights = []
        for h in heads:
            tail = tail_ref[h]
            e = z[h] - cum[h] + tail
            if masked:
                e = jnp.where(strict, e, NEG_BIG)
            weights.append(jnp.exp2(e).astype(BF16))
            tail_ref[h] = tail - cum[h][:1, :]
        for h in heads:
            vt = vt_ref[0, h * HEAD_DIM:(h + 1) * HEAD_DIM, pl.ds(k0, t)]
            acc_ref[h] += _dot(vt, weights[h])

    def live():
        return jnp.max(tail_ref[...]) > -SB_ZERO_EXP

    key_tile(qi, True)

    def cond(carry):
        j, go = carry
        return jnp.logical_and(j >= 0, go)

    def body(carry):
        j, _ = carry
        key_tile(j, False)
        return j - 1, live()

    lax.while_loop(cond, body, (qi - 1, live()))
    out = jnp.concatenate([_head_norm_t(acc_ref[h]) for h in range(n_heads)], axis=0)
    o_ref[0] = (out.T * g_ref[...]).astype(o_ref.dtype)


def _sb_attention(qt, k, vt, gain, *, t):
    b, lp, width = k.shape
    n_heads = width // HEAD_DIM
    idx = jnp.arange(t)
    tri = (idx[None, :] >= idx[:, None]).astype(BF16)
    tri = jnp.concatenate([tri, tri], axis=1)
    return pl.pallas_call(
        functools.partial(_sb_kernel, t=t, n_heads=n_heads),
        grid=(b, lp // t),
        in_specs=[pl.BlockSpec((1, width, t), lambda bi, qi: (bi, 0, qi)),
                  pl.BlockSpec((1, lp, width), lambda bi, qi: (bi, 0, 0)),
                  pl.BlockSpec((1, width, lp), lambda bi, qi: (bi, 0, 0)),
                  pl.BlockSpec((t, 2 * t), lambda bi, qi: (0, 0)),
                  pl.BlockSpec((1, width), lambda bi, qi: (0, 0))],
        out_specs=pl.BlockSpec((1, t, width), lambda bi, qi: (bi, qi, 0)),
        out_shape=jax.ShapeDtypeStruct((b, lp, width), BF16),
        scratch_shapes=[pltpu.VMEM((n_heads, LANES, t), BF16),
                        pltpu.VMEM((n_heads, 1, t), F32),
                        pltpu.VMEM((n_heads, HEAD_DIM, t), F32)],
        compiler_params=pltpu.CompilerParams(
            dimension_semantics=("parallel", "parallel"), vmem_limit_bytes=VMEM_LIMIT),
        name="sb_attention",
    )(qt, k, vt, tri, gain)


def _out_proj_kernel(h_ref, u_ref, up_ref, yf_ref, ys_ref, pw_ref, psc_ref,
                     wop_ref, wof_ref, wos_ref, o_ref, *, tm):
    i = pl.program_id(1)
    u = u_ref[0]
    halo = jnp.where(i == 0, 0.0, up_ref[0])
    x = jnp.concatenate([halo, u], axis=0)
    sums = []
    shift = 1
    for _ in POOL_WINDOWS:
        x = x + pltpu.roll(x, shift, 0)
        sums.append(x[HALO:])
        shift *= 2
    group = lax.broadcasted_iota(jnp.int32, u.shape, 1) // POOL_GROUP
    window_sum = sums[-1]
    window = jnp.full(u.shape, POOL_WINDOWS[-1], jnp.int32)
    for g in range(len(POOL_WINDOWS) - 2, -1, -1):
        window_sum = jnp.where(group == g, sums[g], window_sum)
        window = jnp.where(group == g, POOL_WINDOWS[g], window)
    t1 = i * tm + lax.broadcasted_iota(jnp.int32, u.shape, 0) + 1
    count = jnp.minimum(t1, window).astype(F32)
    d = (window_sum / count - u).astype(BF16)
    y_pool = (_dot(d, pw_ref[...]) * psc_ref[...]).astype(BF16)
    o_ref[0] = (h_ref[0] + _dot(y_pool, wop_ref[...]) + _dot(yf_ref[0], wof_ref[...])
                + _dot(ys_ref[0], wos_ref[...]))


def _out_proj(h, u, yf, ys, pw, psc, wop, wof, wos, *, tm):
    b, lp, d = h.shape
    row3 = lambda w: pl.BlockSpec((1, tm, w), lambda bi, i: (bi, i, 0))
    full = lambda arr: pl.BlockSpec(arr.shape, lambda bi, i: (0,) * arr.ndim)
    halo_spec = pl.BlockSpec((1, HALO, u.shape[2]),
                             lambda bi, i: (bi, jnp.maximum(i * (tm // HALO) - 1, 0), 0))
    return pl.pallas_call(
        functools.partial(_out_proj_kernel, tm=tm),
        grid=(b, lp // tm),
        in_specs=[row3(d), row3(u.shape[2]), halo_spec, row3(yf.shape[2]), row3(ys.shape[2]),
                  full(pw), full(psc), full(wop), full(wof), full(wos)],
        out_specs=row3(d),
        out_shape=jax.ShapeDtypeStruct((b, lp, d), F32),
        compiler_params=pltpu.CompilerParams(
            dimension_semantics=("parallel", "parallel"), vmem_limit_bytes=VMEM_LIMIT),
        name="out_proj",
    )(h, u, u, yf, ys, pw, psc, wop, wof, wos)


def _ffn_kernel(h_ref, g_ref, wg_ref, wu_ref, wd_ref, fg_ref, o_ref, *, chunks, final_norm):
    h = h_ref[...]
    a = _rms(h, g_ref[...]).astype(BF16)
    cw = wg_ref.shape[1] // chunks
    out = h
    for c in range(chunks):
        gate = _dot(a, wg_ref[:, c * cw:(c + 1) * cw])
        up = _dot(a, wu_ref[:, c * cw:(c + 1) * cw])
        act = (gate * jax.nn.sigmoid(gate) * up).astype(BF16)
        out = out + _dot(act, wd_ref[c * cw:(c + 1) * cw, :])
    if final_norm:
        out = _rms(out, fg_ref[...])
    o_ref[...] = out


def _ffn(h, g, wg, wu, wd, fg, *, tm, final_norm):
    n, d = h.shape
    row = pl.BlockSpec((tm, d), lambda i: (i, 0))
    resident = lambda arr: pl.BlockSpec(arr.shape, lambda i: (0,) * arr.ndim,
                                        pipeline_mode=pl.Buffered(1))
    return pl.pallas_call(
        functools.partial(_ffn_kernel, chunks=FF_CHUNKS, final_norm=final_norm),
        grid=(n // tm,),
        in_specs=[row, resident(g), resident(wg), resident(wu), resident(wd), resident(fg)],
        out_specs=row,
        out_shape=jax.ShapeDtypeStruct((n, d), F32),
        compiler_params=pltpu.CompilerParams(
            dimension_semantics=("parallel",), vmem_limit_bytes=VMEM_LIMIT),
        name="ffn",
    )(h, g, wg, wu, wd, fg)


def _block_diag(pool_w):
    groups, cin, cout = pool_w.shape
    out = jnp.zeros((groups * cin, groups * cout), pool_w.dtype)
    for g in range(groups):
        out = out.at[g * cin:(g + 1) * cin, g * cout:(g + 1) * cout].set(pool_w[g])
    return out


def _trunk(x, meta_tokens, norm1, w_in, forget_bias, pool_w, pool_scale, fox_out_gain,
           sb_out_gain, w_out, norm2, w_gate, w_up, w_down, final_norm, *,
           attn_tile, fox_q_blocks, row_tile):
    b, s_len, d = x.shape
    depth = norm1.shape[0]
    l = N_META + s_len
    fox_tq = fox_q_blocks * attn_tile
    step = fox_tq * row_tile // math.gcd(fox_tq, row_tile)
    lp = -(-l // step) * step
    pool_width = pool_scale.shape[1]
    fox_w = fox_out_gain.shape[1]
    fox_heads = fox_w // HEAD_DIM
    assert fox_heads <= PART_STRIDE
    meta = jnp.broadcast_to(meta_tokens[None].astype(x.dtype), (b, N_META, d))
    h = jnp.concatenate([meta, x, jnp.zeros((b, lp - l, d), x.dtype)], axis=1)

    o0 = pool_width
    o1 = o0 + 3 * fox_w
    o2 = o1 + fox_heads
    for i in range(depth):
        w = w_in[i].astype(BF16)
        w_forget = jnp.zeros((d, LANES), BF16).at[:, :fox_heads].set(w[:, o1:o2])
        fb = jnp.zeros((1, LANES), F32).at[0, :fox_heads].set(forget_bias[i].astype(F32))
        u, qft, kf, vft, cc, qst, ks, vst = _in_proj(
            h, norm1[i][None], w[:, :o0], w[:, o0:o1], w_forget, w[:, o2:], fb,
            tm=row_tile, n_heads=fox_heads)
        y_fox = _fox_attention(qft, kf, vft, cc, fox_out_gain[i][None], tq=fox_tq, tk=attn_tile)
        y_sb = _sb_attention(qst, ks, vst, sb_out_gain[i][None], t=attn_tile)
        wo = w_out[i].astype(BF16)
        h = _out_proj(h, u, y_fox, y_sb, _block_diag(pool_w[i]).astype(BF16), pool_scale[i][None],
                      wo[:pool_width], wo[pool_width:pool_width + fox_w], wo[pool_width + fox_w:],
                      tm=row_tile)
        h = _ffn(h.reshape(b * lp, d), norm2[i][None], w_gate[i].astype(BF16), w_up[i].astype(BF16),
                 w_down[i].astype(BF16), final_norm[None], tm=row_tile,
                 final_norm=(i == depth - 1)).reshape(b, lp, d)
    return h[:, N_META:l]


def kernel(x, meta_tokens, norm1, w_in, forget_bias, pool_w, pool_scale, fox_out_gain, sb_out_gain,
           w_out, norm2, w_gate, w_up, w_down, final_norm):
    return _trunk(x, meta_tokens, norm1, w_in, forget_bias, pool_w, pool_scale, fox_out_gain,
                  sb_out_gain, w_out, norm2, w_gate, w_up, w_down, final_norm,
                  attn_tile=ATTN_TILE, fox_q_blocks=FOX_Q_BLOCKS, row_tile=ROW_TILE)
```

```python
import functools
import math

import jax
import jax.numpy as jnp
from jax import lax
from jax.experimental import pallas as pl
from jax.experimental.pallas import tpu as pltpu

HEAD_DIM = 64
N_META = 16
EPS = 1e-6
POOL_WINDOWS = (2, 4, 8, 16)
POOL_GROUP = 64
LANES = 128
BF16_ROWS = 16
HALO = 16
PART_STRIDE = 8
LOG2E = math.log2(math.e)
NEG_BIG = -1e30
ZERO_EXP2 = 160.0
NORM_SLACK = 1.02
VMEM_LIMIT = 56 * 1024 * 1024
F32 = jnp.float32
BF16 = jnp.bfloat16

ATTN_TILE = 256
FOX_Q_BLOCKS = 3
ROW_TILE = 384
FF_CHUNKS = 2


def _dot(a, b):
    return jnp.dot(a, b, preferred_element_type=F32)


def _rms(x, gain):
    ms = jnp.mean(x * x, axis=-1, keepdims=True)
    return x * lax.rsqrt(ms + EPS) * gain


def _split3(x):
    hi = x.astype(BF16)
    r = x - hi.astype(F32)
    mid = r.astype(BF16)
    lo = (r - mid.astype(F32)).astype(BF16)
    return hi, mid, lo


def _head_sq_norms(x, lane0):
    xf = x.astype(F32)
    sq = xf * xf
    d = lax.broadcasted_iota(jnp.int32, (x.shape[1], LANES), 0)
    lane = lax.broadcasted_iota(jnp.int32, (x.shape[1], LANES), 1)
    sel = (lane == lane0 + d // HEAD_DIM).astype(BF16)
    hi = sq.astype(BF16)
    return _dot(hi, sel) + _dot((sq - hi.astype(F32)).astype(BF16), sel)


def _in_proj_kernel(h_ref, g_ref, wu_ref, wf_ref, wl_ref, ws_ref, fb_ref,
                    u_ref, qf_ref, kf_ref, vf_ref, cc_ref, stats_ref, qs_ref, ks_ref, vs_ref,
                    carry_ref, *, tm, width, n_heads):
    i = pl.program_id(1)

    @pl.when(i == 0)
    def _():
        carry_ref[...] = jnp.zeros_like(carry_ref)

    a = _rms(h_ref[0], g_ref[...]).astype(BF16)
    u_ref[0] = _dot(a, wu_ref[...])

    scale = HEAD_DIM ** -0.5
    pf = _dot(a, wf_ref[...])
    qf = pf[:, :width] * (scale * LOG2E)
    kf = pf[:, width:2 * width].astype(BF16)
    qf_ref[0] = qf.T.astype(BF16)
    kf_ref[0] = kf
    vf_ref[0] = pf[:, 2 * width:].T.astype(BF16)
    ps = _dot(a, ws_ref[...])
    qs_ref[0] = (ps[:, :width] * (scale * LOG2E)).T.astype(BF16)
    ks_ref[0] = ps[:, width:2 * width].astype(BF16)
    vs_ref[0] = ps[:, 2 * width:].T.astype(BF16)

    fl = _dot(a, wl_ref[...]) + fb_ref[...]
    lane = lax.broadcasted_iota(jnp.int32, fl.shape, 1)
    log_f = jnp.where(lane < n_heads,
                      (jnp.minimum(fl, 0.0) - jnp.log1p(jnp.exp(-jnp.abs(fl)))) * LOG2E, 0.0)
    row = lax.broadcasted_iota(jnp.int32, (tm, tm), 0)
    col = lax.broadcasted_iota(jnp.int32, (tm, tm), 1)
    tri = (col <= row).astype(BF16)
    hi, mid, lo = _split3(log_f)
    c = _dot(tri, hi) + _dot(tri, mid) + _dot(tri, lo) + carry_ref[:1, :]
    carry_ref[...] = jnp.broadcast_to(c[tm - 1:tm, :], carry_ref.shape)
    hi, mid, lo = _split3(c)
    parts = (hi.astype(F32) + pltpu.roll(mid.astype(F32), PART_STRIDE, 1)
             + pltpu.roll(lo.astype(F32), 2 * PART_STRIDE, 1))
    cc_ref[0] = parts.astype(BF16)
    stats_ref[0] = (c + _head_sq_norms(qf.astype(BF16), PART_STRIDE)
                    + _head_sq_norms(kf, 2 * PART_STRIDE))


def _in_proj(h, g, wu, wf, wl, ws, fb, *, tm, n_heads):
    b, lp, d = h.shape
    width = wf.shape[1] // 3
    row3 = lambda w: pl.BlockSpec((1, tm, w), lambda bi, i: (bi, i, 0))
    col3 = lambda w: pl.BlockSpec((1, w, tm), lambda bi, i: (bi, 0, i))
    full = lambda arr: pl.BlockSpec(arr.shape, lambda bi, i: (0,) * arr.ndim)
    rows = jax.ShapeDtypeStruct((b, lp, width), BF16)
    cols = jax.ShapeDtypeStruct((b, width, lp), BF16)
    return pl.pallas_call(
        functools.partial(_in_proj_kernel, tm=tm, width=width, n_heads=n_heads),
        grid=(b, lp // tm),
        in_specs=[row3(d), full(g), full(wu), full(wf), full(wl), full(ws), full(fb)],
        out_specs=[row3(wu.shape[1]), col3(width), row3(width), col3(width), row3(LANES), row3(LANES),
                   col3(width), row3(width), col3(width)],
        out_shape=[jax.ShapeDtypeStruct((b, lp, wu.shape[1]), F32), cols, rows, cols,
                   jax.ShapeDtypeStruct((b, lp, LANES), BF16),
                   jax.ShapeDtypeStruct((b, lp, LANES), F32), cols, rows, cols],
        scratch_shapes=[pltpu.VMEM((8, LANES), F32)],
        compiler_params=pltpu.CompilerParams(
            dimension_semantics=("parallel", "arbitrary"), vmem_limit_bytes=VMEM_LIMIT),
        name="in_proj",
    )(h, g, wu, wf, wl, ws, fb)


def _fox_plan_kernel(stats_ref, js_ref, *, tq, tk):
    s = stats_ref[0]
    lp = s.shape[0]
    rows_out = js_ref.shape[1]
    lane = lax.broadcasted_iota(jnp.int32, s.shape, 1)
    k_max_sq = pltpu.roll(jnp.max(s, axis=0, keepdims=True), LANES - 2 * PART_STRIDE, 1)
    q_sq = pltpu.roll(s, LANES - PART_STRIDE, 1)
    spread = jnp.where(lane < PART_STRIDE, jnp.sqrt(q_sq * k_max_sq), 0.0)
    reach = s + (2.0 * NORM_SLACK) * spread + ZERO_EXP2
    tiles = [jnp.max(reach[i * tq:(i + 1) * tq], axis=0, keepdims=True) for i in range(lp // tq)]
    tiles.append(jnp.full((rows_out - len(tiles), LANES), -NEG_BIG, F32))
    reach_max = jnp.concatenate(tiles, axis=0)
    n_key_tiles = lp // tk
    c_end = stats_ref[0, pl.ds(tk - 1, n_key_tiles, stride=tk), :]
    count = jnp.zeros(reach_max.shape, jnp.int32)
    for j in range(n_key_tiles):
        count = count + (c_end[j:j + 1, :] > reach_max).astype(jnp.int32)
    js_ref[0] = jnp.minimum(count, pltpu.roll(count, LANES - 1, 1))


def _fox_plan(stats, *, tq, tk):
    b, lp, _ = stats.shape
    rows_out = -(-(lp // tq) // 8) * 8
    return pl.pallas_call(
        functools.partial(_fox_plan_kernel, tq=tq, tk=tk),
        grid=(b,),
        in_specs=[pl.BlockSpec((1, lp, LANES), lambda bi: (bi, 0, 0))],
        out_specs=pl.BlockSpec((1, rows_out, LANES), lambda bi: (bi, 0, 0)),
        out_shape=jax.ShapeDtypeStruct((b, rows_out, LANES), jnp.int32),
        compiler_params=pltpu.CompilerParams(
            dimension_semantics=("parallel",), vmem_limit_bytes=VMEM_LIMIT),
        name="fox_plan",
    )(stats)


def _head_norm_t(o_t):
    ms = jnp.mean(o_t * o_t, axis=0, keepdims=True)
    return o_t * lax.rsqrt(ms + EPS)


def _fox_kernel(first_ref, qt_ref, k_ref, vt_ref, cc_ref, g_ref, o_ref,
                qa_ref, m_ref, alpha_ref, pt_ref, st_ref, acc_ref, *, tq, tk):
    p = pl.program_id(1)
    qi = pl.program_id(2)
    nb = tq // tk
    n_full = nb * qi
    step = (pl.program_id(0) * pl.num_programs(1) + p) * pl.num_programs(2) + qi
    first = jnp.minimum(first_ref[step], n_full)

    qt = qt_ref[0]
    r = lax.broadcasted_iota(jnp.int32, qt.shape, 0)
    zero = jnp.zeros_like(qt)
    for hh in (0, 1):
        own = (r < HEAD_DIM) if hh == 0 else (r >= HEAD_DIM)
        h = 2 * p + hh
        picks = (r == h) | (r == h + PART_STRIDE) | (r == h + 2 * PART_STRIDE)
        qa_ref[hh, :LANES, :] = jnp.where(own, qt, zero)
        qa_ref[hh, LANES:, :] = jnp.where(picks, -1.0, 0.0).astype(BF16)
    m_ref[...] = jnp.full(m_ref.shape, NEG_BIG, F32)
    acc_ref[...] = jnp.zeros(acc_ref.shape, F32)
    ones = jnp.ones((BF16_ROWS, tk), BF16)

    def pieces_from(c0):
        return [(hh, c) for hh in (0, 1) for c in range(c0, tq, tk)]

    def scores(j, pieces):
        k0 = pl.multiple_of(j * tk, tk)
        kk = jnp.concatenate([k_ref[0, pl.ds(k0, tk), :], cc_ref[0, pl.ds(k0, tk), :]], axis=1)
        return [_dot(kk, qa_ref[hh, :, c:c + tk]) for hh, c in pieces]

    def softmax(st, pieces, masked):
        if masked:
            row = lax.broadcasted_iota(jnp.int32, (tk, tk), 0)
            col = lax.broadcasted_iota(jnp.int32, (tk, tk), 1)
            c0 = pieces[0][1]
            st = [jnp.where(row <= col, s, NEG_BIG) if c == c0 else s for s, (_, c) in zip(st, pieces)]
        for s, (hh, c) in zip(st, pieces):
            m_old = m_ref[hh, :, c:c + tk]
            m_new = jnp.maximum(m_old, jnp.max(s, axis=0, keepdims=True))
            m_ref[hh, :, c:c + tk] = m_new
            alpha_ref[hh, :, c:c + tk] = jnp.exp2(m_old - m_new)
            pt_ref[hh, :, c:c + tk] = jnp.exp2(s - m_new).astype(BF16)

    def accumulate(j, pieces):
        k0 = pl.multiple_of(j * tk, tk)
        for hh, c in pieces:
            vt = jnp.concatenate(
                [vt_ref[0, hh * HEAD_DIM:(hh + 1) * HEAD_DIM, pl.ds(k0, tk)], ones], axis=0)
            acc_ref[hh, :, c:c + tk] = (alpha_ref[hh, :, c:c + tk] * acc_ref[hh, :, c:c + tk]
                                        + _dot(vt, pt_ref[hh, :, c:c + tk]))

    every = pieces_from(0)
    alpha_ref[...] = jnp.ones(alpha_ref.shape, F32)
    pt_ref[...] = jnp.zeros(pt_ref.shape, BF16)

    def put_scores(st):
        for s, (hh, c) in zip(st, every):
            st_ref[hh, :, c:c + tk] = s

    def get_scores():
        return [st_ref[hh, :, c:c + tk] for hh, c in every]

    def full_tile(j):
        ahead = scores(j + 1, every)
        accumulate(jnp.maximum(j - 1, 0), every)
        softmax(get_scores(), every, False)
        put_scores(ahead)

    def two_full_tiles(i, carry):
        full_tile(first + 2 * i)
        full_tile(first + 2 * i + 1)
        return carry

    n_tiles = n_full - first
    put_scores(scores(first, every))
    lax.fori_loop(0, n_tiles // 2, two_full_tiles, 0)

    @pl.when(n_tiles % 2 == 1)
    def _():
        full_tile(n_full - 1)

    accumulate(jnp.maximum(n_full - 1, 0), every)
    softmax(get_scores(), every, True)
    accumulate(n_full, every)
    for d in range(1, nb):
        pieces = pieces_from(d * tk)
        softmax(scores(n_full + d, pieces), pieces, True)
        accumulate(n_full + d, pieces)

    out = []
    for hh in (0, 1):
        acc = acc_ref[hh]
        out.append(_head_norm_t(acc[:HEAD_DIM] / acc[HEAD_DIM:HEAD_DIM + 1]))
    o_ref[0] = (jnp.concatenate(out, axis=0).T * g_ref[...]).astype(o_ref.dtype)


def _fox_attention(first, qt, k, vt, cc, gain, *, tq, tk):
    b, lp, width = k.shape
    pairs = width // LANES
    grid_spec = pltpu.PrefetchScalarGridSpec(
        num_scalar_prefetch=1,
        grid=(b, pairs, lp // tq),
        in_specs=[pl.BlockSpec((1, LANES, tq), lambda bi, p, qi, first: (bi, p, qi)),
                  pl.BlockSpec((1, lp, LANES), lambda bi, p, qi, first: (bi, 0, p)),
                  pl.BlockSpec((1, LANES, lp), lambda bi, p, qi, first: (bi, p, 0)),
                  pl.BlockSpec((1, lp, LANES), lambda bi, p, qi, first: (bi, 0, 0)),
                  pl.BlockSpec((1, LANES), lambda bi, p, qi, first: (0, p))],
        out_specs=pl.BlockSpec((1, tq, LANES), lambda bi, p, qi, first: (bi, qi, p)),
        scratch_shapes=[pltpu.VMEM((2, 2 * LANES, tq), BF16),
                        pltpu.VMEM((2, 1, tq), F32),
                        pltpu.VMEM((2, 1, tq), F32),
                        pltpu.VMEM((2, tk, tq), BF16),
                        pltpu.VMEM((2, tk, tq), F32),
                        pltpu.VMEM((2, HEAD_DIM + BF16_ROWS, tq), F32)])
    return pl.pallas_call(
        functools.partial(_fox_kernel, tq=tq, tk=tk),
        grid_spec=grid_spec,
        out_shape=jax.ShapeDtypeStruct((b, lp, width), BF16),
        compiler_params=pltpu.CompilerParams(
            dimension_semantics=("parallel", "parallel", "parallel"), vmem_limit_bytes=VMEM_LIMIT),
        name="fox_attention",
    )(first, qt, k, vt, cc, gain)


def _sb_kernel(qt_ref, k_ref, vt_ref, tri_ref, g_ref, o_ref, qa_ref, tail_ref, acc_ref, *, t, n_heads):
    qi = pl.program_id(1)
    for h in range(n_heads):
        qt = qt_ref[0, h * HEAD_DIM:(h + 1) * HEAD_DIM, :]
        zero = jnp.zeros_like(qt)
        qa_ref[h] = jnp.concatenate([qt, zero] if h % 2 == 0 else [zero, qt], axis=0)
    tail_ref[...] = jnp.zeros(tail_ref.shape, F32)
    acc_ref[...] = jnp.zeros(acc_ref.shape, F32)

    def key_tile(j, masked):
        k0 = pl.multiple_of(j * t, t)
        tri = tri_ref[...]
        heads = range(n_heads)
        if masked:
            row = lax.broadcasted_iota(jnp.int32, (t, t), 0)
            col = lax.broadcasted_iota(jnp.int32, (t, t), 1)
            strict = row < col
        z = [_dot(k_ref[0, pl.ds(k0, t), (h // 2) * LANES:(h // 2 + 1) * LANES], qa_ref[h])
             for h in heads]
        split = []
        for h in heads:
            sp = jnp.maximum(z[h], 0.0) + jnp.log(1.0 + jnp.exp2(-jnp.abs(z[h]))) * LOG2E
            if masked:
                sp = jnp.where(strict, sp, 0.0)
            hi = sp.astype(BF16)
            split.append((hi, (sp - hi.astype(F32)).astype(BF16)))
        cum = [_dot(tri, jnp.concatenate([hi, lo], axis=0)) for hi, lo in split]
        weights = []
        for h in heads:
            tail = tail_ref[h]
            e = z[h] - cum[h] + tail
            if masked:
                e = jnp.where(strict, e, NEG_BIG)
            weights.append(jnp.exp2(e).astype(BF16))
            tail_ref[h] = tail - cum[h][:1, :]
        for h in heads:
            vt = vt_ref[0, h * HEAD_DIM:(h + 1) * HEAD_DIM, pl.ds(k0, t)]
            acc_ref[h] += _dot(vt, weights[h])

    def live():
        return jnp.max(tail_ref[...]) > -ZERO_EXP2

    key_tile(qi, True)

    def cond(carry):
        j, go = carry
        return jnp.logical_and(j >= 0, go)

    def body(carry):
        j, _ = carry
        key_tile(j, False)
        return j - 1, live()

    lax.while_loop(cond, body, (qi - 1, live()))
    out = jnp.concatenate([_head_norm_t(acc_ref[h]) for h in range(n_heads)], axis=0)
    o_ref[0] = (out.T * g_ref[...]).astype(o_ref.dtype)


def _sb_attention(qt, k, vt, gain, *, t):
    b, lp, width = k.shape
    n_heads = width // HEAD_DIM
    idx = jnp.arange(t)
    tri = (idx[None, :] >= idx[:, None]).astype(BF16)
    tri = jnp.concatenate([tri, tri], axis=1)
    return pl.pallas_call(
        functools.partial(_sb_kernel, t=t, n_heads=n_heads),
        grid=(b, lp // t),
        in_specs=[pl.BlockSpec((1, width, t), lambda bi, qi: (bi, 0, qi)),
                  pl.BlockSpec((1, lp, width), lambda bi, qi: (bi, 0, 0)),
                  pl.BlockSpec((1, width, lp), lambda bi, qi: (bi, 0, 0)),
                  pl.BlockSpec((t, 2 * t), lambda bi, qi: (0, 0)),
                  pl.BlockSpec((1, width), lambda bi, qi: (0, 0))],
        out_specs=pl.BlockSpec((1, t, width), lambda bi, qi: (bi, qi, 0)),
        out_shape=jax.ShapeDtypeStruct((b, lp, width), BF16),
        scratch_shapes=[pltpu.VMEM((n_heads, LANES, t), BF16),
                        pltpu.VMEM((n_heads, 1, t), F32),
                        pltpu.VMEM((n_heads, HEAD_DIM, t), F32)],
        compiler_params=pltpu.CompilerParams(
            dimension_semantics=("parallel", "parallel"), vmem_limit_bytes=VMEM_LIMIT),
        name="sb_attention",
    )(qt, k, vt, tri, gain)


def _out_proj_kernel(h_ref, u_ref, up_ref, yf_ref, ys_ref, pw_ref, psc_ref,
                     wop_ref, wof_ref, wos_ref, o_ref, *, tm):
    i = pl.program_id(1)
    u = u_ref[0]
    halo = jnp.where(i == 0, 0.0, up_ref[0])
    x = jnp.concatenate([halo, u], axis=0)
    sums = []
    shift = 1
    for _ in POOL_WINDOWS:
        x = x + pltpu.roll(x, shift, 0)
        sums.append(x[HALO:])
        shift *= 2
    group = lax.broadcasted_iota(jnp.int32, u.shape, 1) // POOL_GROUP
    window_sum = sums[-1]
    window = jnp.full(u.shape, POOL_WINDOWS[-1], jnp.int32)
    for g in range(len(POOL_WINDOWS) - 2, -1, -1):
        window_sum = jnp.where(group == g, sums[g], window_sum)
        window = jnp.where(group == g, POOL_WINDOWS[g], window)
    t1 = i * tm + lax.broadcasted_iota(jnp.int32, u.shape, 0) + 1
    count = jnp.minimum(t1, window).astype(F32)
    d = (window_sum / count - u).astype(BF16)
    y_pool = (_dot(d, pw_ref[...]) * psc_ref[...]).astype(BF16)
    o_ref[0] = (h_ref[0] + _dot(y_pool, wop_ref[...]) + _dot(yf_ref[0], wof_ref[...])
                + _dot(ys_ref[0], wos_ref[...]))


def _out_proj(h, u, yf, ys, pw, psc, wop, wof, wos, *, tm):
    b, lp, d = h.shape
    row3 = lambda w: pl.BlockSpec((1, tm, w), lambda bi, i: (bi, i, 0))
    full = lambda arr: pl.BlockSpec(arr.shape, lambda bi, i: (0,) * arr.ndim)
    halo_spec = pl.BlockSpec((1, HALO, u.shape[2]),
                             lambda bi, i: (bi, jnp.maximum(i * (tm // HALO) - 1, 0), 0))
    return pl.pallas_call(
        functools.partial(_out_proj_kernel, tm=tm),
        grid=(b, lp // tm),
        in_specs=[row3(d), row3(u.shape[2]), halo_spec, row3(yf.shape[2]), row3(ys.shape[2]),
                  full(pw), full(psc), full(wop), full(wof), full(wos)],
        out_specs=row3(d),
        out_shape=jax.ShapeDtypeStruct((b, lp, d), F32),
        compiler_params=pltpu.CompilerParams(
            dimension_semantics=("parallel", "parallel"), vmem_limit_bytes=VMEM_LIMIT),
        name="out_proj",
    )(h, u, u, yf, ys, pw, psc, wop, wof, wos)


def _ffn_kernel(h_ref, g_ref, wg_ref, wu_ref, wd_ref, fg_ref, o_ref, *, chunks, final_norm):
    h = h_ref[...]
    a = _rms(h, g_ref[...]).astype(BF16)
    cw = wg_ref.shape[1] // chunks
    out = h
    for c in range(chunks):
        gate = _dot(a, wg_ref[:, c * cw:(c + 1) * cw])
        up = _dot(a, wu_ref[:, c * cw:(c + 1) * cw])
        act = (gate * jax.nn.sigmoid(gate) * up).astype(BF16)
        out = out + _dot(act, wd_ref[c * cw:(c + 1) * cw, :])
    if final_norm:
        out = _rms(out, fg_ref[...])
    o_ref[...] = out


def _ffn(h, g, wg, wu, wd, fg, *, tm, final_norm):
    n, d = h.shape
    row = pl.BlockSpec((tm, d), lambda i: (i, 0))
    resident = lambda arr: pl.BlockSpec(arr.shape, lambda i: (0,) * arr.ndim,
                                        pipeline_mode=pl.Buffered(1))
    return pl.pallas_call(
        functools.partial(_ffn_kernel, chunks=FF_CHUNKS, final_norm=final_norm),
        grid=(n // tm,),
        in_specs=[row, resident(g), resident(wg), resident(wu), resident(wd), resident(fg)],
        out_specs=row,
        out_shape=jax.ShapeDtypeStruct((n, d), F32),
        compiler_params=pltpu.CompilerParams(
            dimension_semantics=("parallel",), vmem_limit_bytes=VMEM_LIMIT),
        name="ffn",
    )(h, g, wg, wu, wd, fg)


def _block_diag(pool_w):
    groups, cin, cout = pool_w.shape
    out = jnp.zeros((groups * cin, groups * cout), pool_w.dtype)
    for g in range(groups):
        out = out.at[g * cin:(g + 1) * cin, g * cout:(g + 1) * cout].set(pool_w[g])
    return out


def _trunk(x, meta_tokens, norm1, w_in, forget_bias, pool_w, pool_scale, fox_out_gain,
           sb_out_gain, w_out, norm2, w_gate, w_up, w_down, final_norm, *,
           attn_tile, fox_q_blocks, row_tile):
    b, s_len, d = x.shape
    depth = norm1.shape[0]
    l = N_META + s_len
    fox_tq = fox_q_blocks * attn_tile
    step = fox_tq * row_tile // math.gcd(fox_tq, row_tile)
    lp = -(-l // step) * step
    pool_width = pool_scale.shape[1]
    fox_w = fox_out_gain.shape[1]
    fox_heads = fox_w // HEAD_DIM
    assert fox_heads <= PART_STRIDE
    meta = jnp.broadcast_to(meta_tokens[None].astype(x.dtype), (b, N_META, d))
    h = jnp.concatenate([meta, x, jnp.zeros((b, lp - l, d), x.dtype)], axis=1)

    o0 = pool_width
    o1 = o0 + 3 * fox_w
    o2 = o1 + fox_heads
    for i in range(depth):
        w = w_in[i].astype(BF16)
        w_forget = jnp.zeros((d, LANES), BF16).at[:, :fox_heads].set(w[:, o1:o2])
        fb = jnp.zeros((1, LANES), F32).at[0, :fox_heads].set(forget_bias[i].astype(F32))
        u, qft, kf, vft, cc, stats, qst, ks, vst = _in_proj(
            h, norm1[i][None], w[:, :o0], w[:, o0:o1], w_forget, w[:, o2:], fb,
            tm=row_tile, n_heads=fox_heads)
        plan = _fox_plan(stats, tq=fox_tq, tk=attn_tile)
        first = plan[:, :lp // fox_tq, 0:fox_heads:2].transpose(0, 2, 1).reshape(-1)
        y_fox = _fox_attention(first, qft, kf, vft, cc, fox_out_gain[i][None], tq=fox_tq, tk=attn_tile)
        y_sb = _sb_attention(qst, ks, vst, sb_out_gain[i][None], t=attn_tile)
        wo = w_out[i].astype(BF16)
        h = _out_proj(h, u, y_fox, y_sb, _block_diag(pool_w[i]).astype(BF16), pool_scale[i][None],
                      wo[:pool_width], wo[pool_width:pool_width + fox_w], wo[pool_width + fox_w:],
                      tm=row_tile)
        h = _ffn(h.reshape(b * lp, d), norm2[i][None], w_gate[i].astype(BF16), w_up[i].astype(BF16),
                 w_down[i].astype(BF16), final_norm[None], tm=row_tile,
                 final_norm=(i == depth - 1)).reshape(b, lp, d)
    return h[:, N_META:l]


def kernel(x, meta_tokens, norm1, w_in, forget_bias, pool_w, pool_scale, fox_out_gain, sb_out_gain,
           w_out, norm2, w_gate, w_up, w_down, final_norm):
    return _trunk(x, meta_tokens, norm1, w_in, forget_bias, pool_w, pool_scale, fox_out_gain,
                  sb_out_gain, w_out, norm2, w_gate, w_up, w_down, final_norm,
                  attn_tile=ATTN_TILE, fox_q_blocks=FOX_Q_BLOCKS, row_tile=ROW_TILE)
```

```python
import functools
import math

import jax
import jax.numpy as jnp
from jax import lax
from jax.experimental import pallas as pl
from jax.experimental.pallas import tpu as pltpu

HEAD_DIM = 64
N_META = 16
EPS = 1e-6
POOL_WINDOWS = (2, 4, 8, 16)
POOL_GROUP = 64
LANES = 128
BF16_ROWS = 16
HALO = 16
PART_STRIDE = 8
LOG2E = math.log2(math.e)
NEG_BIG = -1e30
ZERO_EXP2 = 160.0
NORM_SLACK = 1.02
VMEM_LIMIT = 56 * 1024 * 1024
F32 = jnp.float32
BF16 = jnp.bfloat16

ATTN_TILE = 256
FOX_Q_BLOCKS = 3
ROW_TILE = 384
FFN_LAST_TILE = 512
FF_CHUNKS = 2


def _dot(a, b):
    return jnp.dot(a, b, preferred_element_type=F32)


def _rms(x, gain):
    ms = jnp.mean(x * x, axis=-1, keepdims=True)
    return x * lax.rsqrt(ms + EPS) * gain


def _split3(x):
    hi = x.astype(BF16)
    r = x - hi.astype(F32)
    mid = r.astype(BF16)
    lo = (r - mid.astype(F32)).astype(BF16)
    return hi, mid, lo


def _pack_parts(x):
    hi, mid, lo = _split3(x)
    return (hi.astype(F32) + pltpu.roll(mid.astype(F32), PART_STRIDE, 1)
            + pltpu.roll(lo.astype(F32), 2 * PART_STRIDE, 1)).astype(BF16)


def _head_sq_norms(q, k):
    width = q.shape[1]
    sq = jnp.concatenate([(x.astype(F32) * x.astype(F32)).astype(BF16) for x in (q, k)], axis=1)
    d = lax.broadcasted_iota(jnp.int32, (2 * width, LANES), 0)
    lane = lax.broadcasted_iota(jnp.int32, (2 * width, LANES), 1)
    target = jnp.where(d < width, PART_STRIDE + d // HEAD_DIM, 2 * PART_STRIDE + (d - width) // HEAD_DIM)
    return _dot(sq, (lane == target).astype(BF16))


def _in_proj_kernel(h_ref, g_ref, wu_ref, wf_ref, ws_ref, fb_ref,
                    u_ref, qf_ref, kf_ref, vf_ref, cc_ref, stats_ref, qs_ref, ks_ref, vs_ref,
                    carry_ref, *, tm, width, n_heads):
    i = pl.program_id(1)

    @pl.when(i == 0)
    def _():
        carry_ref[...] = jnp.zeros_like(carry_ref)

    a = _rms(h_ref[0], g_ref[...]).astype(BF16)
    u_ref[0] = _dot(a, wu_ref[...])

    scale = HEAD_DIM ** -0.5
    pf = _dot(a, wf_ref[...])
    qf = pf[:, :width] * (scale * LOG2E)
    kf = pf[:, width:2 * width].astype(BF16)
    qf_ref[0] = qf.T.astype(BF16)
    kf_ref[0] = kf
    vf_ref[0] = pf[:, 2 * width:3 * width].T.astype(BF16)
    ps = _dot(a, ws_ref[...])
    qs_ref[0] = (ps[:, :width] * (scale * LOG2E)).T.astype(BF16)
    ks_ref[0] = ps[:, width:2 * width].astype(BF16)
    vs_ref[0] = ps[:, 2 * width:].T.astype(BF16)

    fl = pf[:, 3 * width:] + fb_ref[...]
    lane = lax.broadcasted_iota(jnp.int32, fl.shape, 1)
    log_f = jnp.where(lane < n_heads,
                      (jnp.minimum(fl, 0.0) - jnp.log1p(jnp.exp(-jnp.abs(fl)))) * LOG2E, 0.0)
    row = lax.broadcasted_iota(jnp.int32, (tm, tm), 0)
    col = lax.broadcasted_iota(jnp.int32, (tm, tm), 1)
    sums = _dot((col <= row).astype(BF16), _pack_parts(log_f))
    sums = sums + pltpu.roll(sums, LANES - PART_STRIDE, 1) + pltpu.roll(sums, LANES - 2 * PART_STRIDE, 1)
    c = jnp.where(lane < PART_STRIDE, sums, 0.0) + carry_ref[:1, :]
    carry_ref[...] = jnp.broadcast_to(c[tm - 1:tm, :], carry_ref.shape)
    cc_ref[0] = _pack_parts(c)
    stats_ref[0] = c + _head_sq_norms(qf.astype(BF16), kf)


def _in_proj(h, g, wu, wf, ws, fb, *, tm, n_heads):
    b, lp, d = h.shape
    width = ws.shape[1] // 3
    row3 = lambda w: pl.BlockSpec((1, tm, w), lambda bi, i: (bi, i, 0))
    col3 = lambda w: pl.BlockSpec((1, w, tm), lambda bi, i: (bi, 0, i))
    full = lambda arr: pl.BlockSpec(arr.shape, lambda bi, i: (0,) * arr.ndim)
    rows = jax.ShapeDtypeStruct((b, lp, width), BF16)
    cols = jax.ShapeDtypeStruct((b, width, lp), BF16)
    return pl.pallas_call(
        functools.partial(_in_proj_kernel, tm=tm, width=width, n_heads=n_heads),
        grid=(b, lp // tm),
        in_specs=[row3(d), full(g), full(wu), full(wf), full(ws), full(fb)],
        out_specs=[row3(wu.shape[1]), col3(width), row3(width), col3(width), row3(LANES), row3(LANES),
                   col3(width), row3(width), col3(width)],
        out_shape=[jax.ShapeDtypeStruct((b, lp, wu.shape[1]), F32), cols, rows, cols,
                   jax.ShapeDtypeStruct((b, lp, LANES), BF16),
                   jax.ShapeDtypeStruct((b, lp, LANES), F32), cols, rows, cols],
        scratch_shapes=[pltpu.VMEM((8, LANES), F32)],
        compiler_params=pltpu.CompilerParams(
            dimension_semantics=("parallel", "arbitrary"), vmem_limit_bytes=VMEM_LIMIT),
        name="in_proj",
    )(h, g, wu, wf, ws, fb)


def _fox_plan_kernel(stats_ref, js_ref, *, tq, tk):
    s = stats_ref[0]
    lp = s.shape[0]
    rows_out = js_ref.shape[1]
    lane = lax.broadcasted_iota(jnp.int32, s.shape, 1)
    k_max_sq = pltpu.roll(jnp.max(s, axis=0, keepdims=True), LANES - 2 * PART_STRIDE, 1)
    q_sq = pltpu.roll(s, LANES - PART_STRIDE, 1)
    spread = jnp.where(lane < PART_STRIDE, jnp.sqrt(q_sq * k_max_sq), 0.0)
    reach = s + (2.0 * NORM_SLACK) * spread + ZERO_EXP2
    tiles = [jnp.max(reach[i * tq:(i + 1) * tq], axis=0, keepdims=True) for i in range(lp // tq)]
    tiles.append(jnp.full((rows_out - len(tiles), LANES), -NEG_BIG, F32))
    reach_max = jnp.concatenate(tiles, axis=0)
    n_key_tiles = lp // tk
    c_end = stats_ref[0, pl.ds(tk - 1, n_key_tiles, stride=tk), :]
    count = jnp.zeros(reach_max.shape, jnp.int32)
    for j in range(n_key_tiles):
        count = count + (c_end[j:j + 1, :] > reach_max).astype(jnp.int32)
    js_ref[0] = jnp.minimum(count, pltpu.roll(count, LANES - 1, 1))


def _fox_plan(stats, *, tq, tk):
    b, lp, _ = stats.shape
    rows_out = -(-(lp // tq) // 8) * 8
    return pl.pallas_call(
        functools.partial(_fox_plan_kernel, tq=tq, tk=tk),
        grid=(b,),
        in_specs=[pl.BlockSpec((1, lp, LANES), lambda bi: (bi, 0, 0))],
        out_specs=pl.BlockSpec((1, rows_out, LANES), lambda bi: (bi, 0, 0)),
        out_shape=jax.ShapeDtypeStruct((b, rows_out, LANES), jnp.int32),
        compiler_params=pltpu.CompilerParams(
            dimension_semantics=("parallel",), vmem_limit_bytes=VMEM_LIMIT),
        name="fox_plan",
    )(stats)


def _head_norm_t(o_t):
    ms = jnp.mean(o_t * o_t, axis=0, keepdims=True)
    return o_t * lax.rsqrt(ms + EPS)


def _fox_kernel(first_ref, qt_ref, k_ref, vt_ref, cc_ref, g_ref, o_ref,
                qa_ref, m_ref, alpha_ref, pt_ref, st_ref, acc_ref, *, tq, tk):
    p = pl.program_id(1)
    qi = pl.program_id(2)
    nb = tq // tk
    n_full = nb * qi
    step = (pl.program_id(0) * pl.num_programs(1) + p) * pl.num_programs(2) + qi
    first = jnp.minimum(first_ref[step], n_full)

    qt = qt_ref[0]
    r = lax.broadcasted_iota(jnp.int32, qt.shape, 0)
    zero = jnp.zeros_like(qt)
    for hh in (0, 1):
        own = (r < HEAD_DIM) if hh == 0 else (r >= HEAD_DIM)
        h = 2 * p + hh
        picks = (r == h) | (r == h + PART_STRIDE) | (r == h + 2 * PART_STRIDE)
        qa_ref[hh, :LANES, :] = jnp.where(own, qt, zero)
        qa_ref[hh, LANES:, :] = jnp.where(picks, -1.0, 0.0).astype(BF16)
    m_ref[...] = jnp.full(m_ref.shape, NEG_BIG, F32)
    acc_ref[...] = jnp.zeros(acc_ref.shape, F32)
    ones = jnp.ones((BF16_ROWS, tk), BF16)

    def pieces_from(c0):
        return [(hh, c) for hh in (0, 1) for c in range(c0, tq, tk)]

    def scores(j, pieces):
        k0 = pl.multiple_of(j * tk, tk)
        kk = jnp.concatenate([k_ref[0, pl.ds(k0, tk), :], cc_ref[0, pl.ds(k0, tk), :]], axis=1)
        return [_dot(kk, qa_ref[hh, :, c:c + tk]) for hh, c in pieces]

    def softmax(st, pieces, masked):
        if masked:
            row = lax.broadcasted_iota(jnp.int32, (tk, tk), 0)
            col = lax.broadcasted_iota(jnp.int32, (tk, tk), 1)
            c0 = pieces[0][1]
            st = [jnp.where(row <= col, s, NEG_BIG) if c == c0 else s for s, (_, c) in zip(st, pieces)]
        weights = []
        for s, (hh, c) in zip(st, pieces):
            m_old = m_ref[hh, :, c:c + tk]
            m_new = jnp.maximum(m_old, jnp.max(s, axis=0, keepdims=True))
            m_ref[hh, :, c:c + tk] = m_new
            weights.append((jnp.exp2(s - m_new).astype(BF16), jnp.exp2(m_old - m_new)))
        return weights

    def accumulate(j, pieces, weights):
        k0 = pl.multiple_of(j * tk, tk)
        for (hh, c), (pt, alpha) in zip(pieces, weights):
            vt = jnp.concatenate(
                [vt_ref[0, hh * HEAD_DIM:(hh + 1) * HEAD_DIM, pl.ds(k0, tk)], ones], axis=0)
            acc_ref[hh, :, c:c + tk] = alpha * acc_ref[hh, :, c:c + tk] + _dot(vt, pt)

    every = pieces_from(0)
    alpha_ref[...] = jnp.ones(alpha_ref.shape, F32)
    pt_ref[...] = jnp.zeros(pt_ref.shape, BF16)

    def put_scores(st):
        for s, (hh, c) in zip(st, every):
            st_ref[hh, :, c:c + tk] = s

    def get_scores():
        return [st_ref[hh, :, c:c + tk] for hh, c in every]

    def put_weights(weights):
        for (pt, alpha), (hh, c) in zip(weights, every):
            pt_ref[hh, :, c:c + tk] = pt
            alpha_ref[hh, :, c:c + tk] = alpha

    def get_weights():
        return [(pt_ref[hh, :, c:c + tk], alpha_ref[hh, :, c:c + tk]) for hh, c in every]

    def full_tile(j, carry):
        ahead = scores(j + 1, every)
        accumulate(jnp.maximum(j - 1, 0), every, get_weights())
        put_weights(softmax(get_scores(), every, False))
        put_scores(ahead)
        return carry

    put_scores(scores(first, every))
    lax.fori_loop(first, n_full, full_tile, 0)
    diagonal = [pieces_from(d * tk) for d in range(nb)]
    st = [get_scores()] + [scores(n_full + d, diagonal[d]) for d in range(1, nb)]
    accumulate(jnp.maximum(n_full - 1, 0), every, get_weights())
    weights = [softmax(st[d], diagonal[d], True) for d in range(nb)]
    for d in range(nb):
        accumulate(n_full + d, diagonal[d], weights[d])

    out = []
    for hh in (0, 1):
        acc = acc_ref[hh]
        out.append(_head_norm_t(acc[:HEAD_DIM] / acc[HEAD_DIM:HEAD_DIM + 1]))
    o_ref[0] = (jnp.concatenate(out, axis=0).T * g_ref[...]).astype(o_ref.dtype)


def _fox_attention(first, qt, k, vt, cc, gain, *, tq, tk):
    b, lp, width = k.shape
    pairs = width // LANES
    grid_spec = pltpu.PrefetchScalarGridSpec(
        num_scalar_prefetch=1,
        grid=(b, pairs, lp // tq),
        in_specs=[pl.BlockSpec((1, LANES, tq), lambda bi, p, qi, first: (bi, p, qi)),
                  pl.BlockSpec((1, lp, LANES), lambda bi, p, qi, first: (bi, 0, p)),
                  pl.BlockSpec((1, LANES, lp), lambda bi, p, qi, first: (bi, p, 0)),
                  pl.BlockSpec((1, lp, LANES), lambda bi, p, qi, first: (bi, 0, 0)),
                  pl.BlockSpec((1, LANES), lambda bi, p, qi, first: (0, p))],
        out_specs=pl.BlockSpec((1, tq, LANES), lambda bi, p, qi, first: (bi, qi, p)),
        scratch_shapes=[pltpu.VMEM((2, 2 * LANES, tq), BF16),
                        pltpu.VMEM((2, 1, tq), F32),
                        pltpu.VMEM((2, 1, tq), F32),
                        pltpu.VMEM((2, tk, tq), BF16),
                        pltpu.VMEM((2, tk, tq), F32),
                        pltpu.VMEM((2, HEAD_DIM + BF16_ROWS, tq), F32)])
    return pl.pallas_call(
        functools.partial(_fox_kernel, tq=tq, tk=tk),
        grid_spec=grid_spec,
        out_shape=jax.ShapeDtypeStruct((b, lp, width), BF16),
        compiler_params=pltpu.CompilerParams(
            dimension_semantics=("parallel", "parallel", "parallel"), vmem_limit_bytes=VMEM_LIMIT),
        name="fox_attention",
    )(first, qt, k, vt, cc, gain)


def _sb_kernel(qt_ref, k_ref, vt_ref, tri_ref, g_ref, o_ref, qa_ref, tail_ref, acc_ref, *, t, n_heads):
    qi = pl.program_id(1)
    for h in range(n_heads):
        qt = qt_ref[0, h * HEAD_DIM:(h + 1) * HEAD_DIM, :]
        zero = jnp.zeros_like(qt)
        qa_ref[h] = jnp.concatenate([qt, zero] if h % 2 == 0 else [zero, qt], axis=0)
    tail_ref[...] = jnp.zeros(tail_ref.shape, F32)
    acc_ref[...] = jnp.zeros(acc_ref.shape, F32)

    def key_tile(j, masked):
        k0 = pl.multiple_of(j * t, t)
        tri = tri_ref[...]
        heads = range(n_heads)
        if masked:
            row = lax.broadcasted_iota(jnp.int32, (t, t), 0)
            col = lax.broadcasted_iota(jnp.int32, (t, t), 1)
            strict = row < col
        z = [_dot(k_ref[0, pl.ds(k0, t), (h // 2) * LANES:(h // 2 + 1) * LANES], qa_ref[h])
             for h in heads]
        split = []
        for h in heads:
            sp = jnp.maximum(z[h], 0.0) + jnp.log(1.0 + jnp.exp2(-jnp.abs(z[h]))) * LOG2E
            if masked:
                sp = jnp.where(strict, sp, 0.0)
            hi = sp.astype(BF16)
            split.append((hi, (sp - hi.astype(F32)).astype(BF16)))
        cum = [_dot(tri, jnp.concatenate([hi, lo], axis=0)) for hi, lo in split]
        weights = []
        for h in heads:
            tail = tail_ref[h]
            e = z[h] - cum[h] + tail
            if masked:
                e = jnp.where(strict, e, NEG_BIG)
            weights.append(jnp.exp2(e).astype(BF16))
            tail_ref[h] = tail - cum[h][:1, :]
        for h in heads:
            vt = vt_ref[0, h * HEAD_DIM:(h + 1) * HEAD_DIM, pl.ds(k0, t)]
            acc_ref[h] += _dot(vt, weights[h])

    def live():
        return jnp.max(tail_ref[...]) > -ZERO_EXP2

    key_tile(qi, True)

    def cond(carry):
        j, go = carry
        return jnp.logical_and(j >= 0, go)

    def body(carry):
        j, _ = carry
        key_tile(j, False)
        return j - 1, live()

    lax.while_loop(cond, body, (qi - 1, live()))
    out = jnp.concatenate([_head_norm_t(acc_ref[h]) for h in range(n_heads)], axis=0)
    o_ref[0] = (out.T * g_ref[...]).astype(o_ref.dtype)


def _sb_attention(qt, k, vt, gain, *, t):
    b, lp, width = k.shape
    n_heads = width // HEAD_DIM
    idx = jnp.arange(t)
    tri = (idx[None, :] >= idx[:, None]).astype(BF16)
    tri = jnp.concatenate([tri, tri], axis=1)
    return pl.pallas_call(
        functools.partial(_sb_kernel, t=t, n_heads=n_heads),
        grid=(b, lp // t),
        in_specs=[pl.BlockSpec((1, width, t), lambda bi, qi: (bi, 0, qi)),
                  pl.BlockSpec((1, lp, width), lambda bi, qi: (bi, 0, 0)),
                  pl.BlockSpec((1, width, lp), lambda bi, qi: (bi, 0, 0)),
                  pl.BlockSpec((t, 2 * t), lambda bi, qi: (0, 0)),
                  pl.BlockSpec((1, width), lambda bi, qi: (0, 0))],
        out_specs=pl.BlockSpec((1, t, width), lambda bi, qi: (bi, qi, 0)),
        out_shape=jax.ShapeDtypeStruct((b, lp, width), BF16),
        scratch_shapes=[pltpu.VMEM((n_heads, LANES, t), BF16),
                        pltpu.VMEM((n_heads, 1, t), F32),
                        pltpu.VMEM((n_heads, HEAD_DIM, t), F32)],
        compiler_params=pltpu.CompilerParams(
            dimension_semantics=("parallel", "parallel"), vmem_limit_bytes=VMEM_LIMIT),
        name="sb_attention",
    )(qt, k, vt, tri, gain)


def _out_proj_kernel(h_ref, u_ref, up_ref, yf_ref, ys_ref, pw_ref, psc_ref, wo_ref, o_ref, *, tm):
    i = pl.program_id(1)
    u = u_ref[0]
    halo = jnp.where(i == 0, 0.0, up_ref[0])
    x = jnp.concatenate([halo, u], axis=0)
    sums = []
    shift = 1
    for _ in POOL_WINDOWS:
        x = x + pltpu.roll(x, shift, 0)
        sums.append(x[HALO:])
        shift *= 2
    group = lax.broadcasted_iota(jnp.int32, u.shape, 1) // POOL_GROUP
    window_sum = sums[-1]
    window = jnp.full(u.shape, POOL_WINDOWS[-1], jnp.int32)
    for g in range(len(POOL_WINDOWS) - 2, -1, -1):
        window_sum = jnp.where(group == g, sums[g], window_sum)
        window = jnp.where(group == g, POOL_WINDOWS[g], window)
    t1 = i * tm + lax.broadcasted_iota(jnp.int32, u.shape, 0) + 1
    count = jnp.minimum(t1, window).astype(F32)
    d = (window_sum / count - u).astype(BF16)
    y_pool = (_dot(d, pw_ref[...]) * psc_ref[...]).astype(BF16)
    y = jnp.concatenate([y_pool, yf_ref[0], ys_ref[0]], axis=1)
    o_ref[0] = h_ref[0] + _dot(y, wo_ref[...])


def _out_proj(h, u, yf, ys, pw, psc, wo, *, tm):
    b, lp, d = h.shape
    row3 = lambda w: pl.BlockSpec((1, tm, w), lambda bi, i: (bi, i, 0))
    full = lambda arr: pl.BlockSpec(arr.shape, lambda bi, i: (0,) * arr.ndim)
    halo_spec = pl.BlockSpec((1, HALO, u.shape[2]),
                             lambda bi, i: (bi, jnp.maximum(i * (tm // HALO) - 1, 0), 0))
    return pl.pallas_call(
        functools.partial(_out_proj_kernel, tm=tm),
        grid=(b, lp // tm),
        in_specs=[row3(d), row3(u.shape[2]), halo_spec, row3(yf.shape[2]), row3(ys.shape[2]),
                  full(pw), full(psc), full(wo)],
        out_specs=row3(d),
        out_shape=jax.ShapeDtypeStruct((b, lp, d), F32),
        compiler_params=pltpu.CompilerParams(
            dimension_semantics=("parallel", "parallel"), vmem_limit_bytes=VMEM_LIMIT),
        name="out_proj",
    )(h, u, u, yf, ys, pw, psc, wo)


def _ffn_kernel(h_ref, g_ref, wg_ref, wu_ref, wd_ref, fg_ref, o_ref, *, chunks, final_norm):
    h = h_ref[0]
    a = _rms(h, g_ref[...]).astype(BF16)
    cw = wg_ref.shape[1] // chunks
    out = h
    for c in range(chunks):
        gate = _dot(a, wg_ref[:, c * cw:(c + 1) * cw])
        up = _dot(a, wu_ref[:, c * cw:(c + 1) * cw])
        act = (gate * jax.nn.sigmoid(gate) * up).astype(BF16)
        out = out + _dot(act, wd_ref[c * cw:(c + 1) * cw, :])
    if final_norm:
        out = _rms(out, fg_ref[...])
    o_ref[0] = out


def _ffn(h, g, wg, wu, wd, fg, *, tm, first_row, n_rows, final_norm):
    b, _, d = h.shape
    rows_in = pl.BlockSpec((pl.Element(1), pl.Element(tm), pl.Element(d)),
                           lambda bi, i: (bi, pl.multiple_of(first_row + i * tm, 8), 0))
    rows_out = pl.BlockSpec((1, tm, d), lambda bi, i: (bi, i, 0))
    resident = lambda arr: pl.BlockSpec(arr.shape, lambda bi, i: (0,) * arr.ndim,
                                        pipeline_mode=pl.Buffered(1))
    return pl.pallas_call(
        functools.partial(_ffn_kernel, chunks=FF_CHUNKS, final_norm=final_norm),
        grid=(b, n_rows // tm),
        in_specs=[rows_in, resident(g), resident(wg), resident(wu), resident(wd), resident(fg)],
        out_specs=rows_out,
        out_shape=jax.ShapeDtypeStruct((b, n_rows, d), F32),
        compiler_params=pltpu.CompilerParams(
            dimension_semantics=("parallel", "parallel"), vmem_limit_bytes=VMEM_LIMIT),
        name="ffn",
    )(h, g, wg, wu, wd, fg)


def _largest_tile(n, limit):
    return max(t for t in range(8, limit + 1, 8) if n % t == 0)


def _block_diag(pool_w):
    groups, cin, cout = pool_w.shape
    out = jnp.zeros((groups * cin, groups * cout), pool_w.dtype)
    for g in range(groups):
        out = out.at[g * cin:(g + 1) * cin, g * cout:(g + 1) * cout].set(pool_w[g])
    return out


def _trunk(x, meta_tokens, norm1, w_in, forget_bias, pool_w, pool_scale, fox_out_gain,
           sb_out_gain, w_out, norm2, w_gate, w_up, w_down, final_norm, *,
           attn_tile, fox_q_blocks, row_tile):
    b, s_len, d = x.shape
    depth = norm1.shape[0]
    l = N_META + s_len
    fox_tq = fox_q_blocks * attn_tile
    step = fox_tq * row_tile // math.gcd(fox_tq, row_tile)
    lp = -(-l // step) * step
    pool_width = pool_scale.shape[1]
    fox_w = fox_out_gain.shape[1]
    fox_heads = fox_w // HEAD_DIM
    assert fox_heads <= PART_STRIDE
    meta = jnp.broadcast_to(meta_tokens[None].astype(x.dtype), (b, N_META, d))
    h = jnp.concatenate([meta, x, jnp.zeros((b, lp - l, d), x.dtype)], axis=1)

    o0 = pool_width
    o1 = o0 + 3 * fox_w
    o2 = o1 + fox_heads
    for i in range(depth):
        w = w_in[i].astype(BF16)
        w_fox = jnp.zeros((d, 3 * fox_w + LANES), BF16).at[:, :o2 - o0].set(w[:, o0:o2])
        fb = jnp.zeros((1, LANES), F32).at[0, :fox_heads].set(forget_bias[i].astype(F32))
        u, qft, kf, vft, cc, stats, qst, ks, vst = _in_proj(
            h, norm1[i][None], w[:, :o0], w_fox, w[:, o2:], fb, tm=row_tile, n_heads=fox_heads)
        plan = _fox_plan(stats, tq=fox_tq, tk=attn_tile)
        first = plan[:, :lp // fox_tq, 0:fox_heads:2].transpose(0, 2, 1).reshape(-1)
        y_fox = _fox_attention(first, qft, kf, vft, cc, fox_out_gain[i][None], tq=fox_tq, tk=attn_tile)
        y_sb = _sb_attention(qst, ks, vst, sb_out_gain[i][None], t=attn_tile)
        h = _out_proj(h, u, y_fox, y_sb, _block_diag(pool_w[i]).astype(BF16), pool_scale[i][None],
                      w_out[i].astype(BF16), tm=row_tile)
        last = i == depth - 1
        rows = dict(tm=_largest_tile(s_len, FFN_LAST_TILE), first_row=N_META, n_rows=s_len) if last \
            else dict(tm=row_tile, first_row=0, n_rows=lp)
        h = _ffn(h, norm2[i][None], w_gate[i].astype(BF16), w_up[i].astype(BF16),
                 w_down[i].astype(BF16), final_norm[None], final_norm=last, **rows)
    return h


def kernel(x, meta_tokens, norm1, w_in, forget_bias, pool_w, pool_scale, fox_out_gain, sb_out_gain,
           w_out, norm2, w_gate, w_up, w_down, final_norm):
    return _trunk(x, meta_tokens, norm1, w_in, forget_bias, pool_w, pool_scale, fox_out_gain,
                  sb_out_gain, w_out, norm2, w_gate, w_up, w_down, final_norm,
                  attn_tile=ATTN_TILE, fox_q_blocks=FOX_Q_BLOCKS, row_tile=ROW_TILE)
```

```python
import functools
import math

import jax
import jax.numpy as jnp
from jax import lax
from jax.experimental import pallas as pl
from jax.experimental.pallas import tpu as pltpu

HEAD_DIM = 64
N_META = 16
EPS = 1e-6
POOL_WINDOWS = (2, 4, 8, 16)
POOL_GROUP = 64
LANES = 128
BF16_ROWS = 16
HALO = 16
PART_STRIDE = 8
LOG2E = math.log2(math.e)
NEG_BIG = -1e30
ZERO_EXP2 = 160.0
NORM_SLACK = 1.02
VMEM_LIMIT = 56 * 1024 * 1024
F32 = jnp.float32
BF16 = jnp.bfloat16

ATTN_TILE = 256
FOX_Q_BLOCKS = 3
ROW_TILE = 768
FFN_LAST_TILE = 512
FF_CHUNKS = 11


def _dot(a, b):
    return jnp.dot(a, b, preferred_element_type=F32)


def _rms(x, gain):
    ms = jnp.mean(x * x, axis=-1, keepdims=True)
    return x * lax.rsqrt(ms + EPS) * gain


def _split3(x):
    hi = x.astype(BF16)
    r = x - hi.astype(F32)
    mid = r.astype(BF16)
    lo = (r - mid.astype(F32)).astype(BF16)
    return hi, mid, lo


def _pack_parts(x):
    hi, mid, lo = _split3(x)
    return (hi.astype(F32) + pltpu.roll(mid.astype(F32), PART_STRIDE, 1)
            + pltpu.roll(lo.astype(F32), 2 * PART_STRIDE, 1)).astype(BF16)


def _head_sq_norms(q, k):
    width = q.shape[1]
    sq = jnp.concatenate([(x.astype(F32) * x.astype(F32)).astype(BF16) for x in (q, k)], axis=1)
    d = lax.broadcasted_iota(jnp.int32, (2 * width, LANES), 0)
    lane = lax.broadcasted_iota(jnp.int32, (2 * width, LANES), 1)
    target = jnp.where(d < width, PART_STRIDE + d // HEAD_DIM, 2 * PART_STRIDE + (d - width) // HEAD_DIM)
    return _dot(sq, (lane == target).astype(BF16))


def _in_proj_kernel(h_ref, g_ref, wu_ref, wf_ref, ws_ref, fb_ref,
                    u_ref, qf_ref, kf_ref, vf_ref, cc_ref, stats_ref, qs_ref, ks_ref, vs_ref,
                    carry_ref, *, tm, width, n_heads):
    i = pl.program_id(1)

    @pl.when(i == 0)
    def _():
        carry_ref[...] = jnp.zeros_like(carry_ref)

    a = _rms(h_ref[0], g_ref[...]).astype(BF16)
    u_ref[0] = _dot(a, wu_ref[...])

    scale = HEAD_DIM ** -0.5
    pf = _dot(a, wf_ref[...])
    qf = pf[:, :width] * (scale * LOG2E)
    kf = pf[:, width:2 * width].astype(BF16)
    qf_ref[0] = qf.T.astype(BF16)
    kf_ref[0] = kf
    vf_ref[0] = pf[:, 2 * width:3 * width].T.astype(BF16)
    ps = _dot(a, ws_ref[...])
    qs_ref[0] = (ps[:, :width] * (scale * LOG2E)).T.astype(BF16)
    ks_ref[0] = ps[:, width:2 * width].astype(BF16)
    vs_ref[0] = ps[:, 2 * width:].T.astype(BF16)

    fl = pf[:, 3 * width:] + fb_ref[...]
    lane = lax.broadcasted_iota(jnp.int32, fl.shape, 1)
    log_f = jnp.where(lane < n_heads,
                      (jnp.minimum(fl, 0.0) - jnp.log1p(jnp.exp(-jnp.abs(fl)))) * LOG2E, 0.0)
    row = lax.broadcasted_iota(jnp.int32, (tm, tm), 0)
    col = lax.broadcasted_iota(jnp.int32, (tm, tm), 1)
    sums = _dot((col <= row).astype(BF16), _pack_parts(log_f))
    sums = sums + pltpu.roll(sums, LANES - PART_STRIDE, 1) + pltpu.roll(sums, LANES - 2 * PART_STRIDE, 1)
    c = jnp.where(lane < PART_STRIDE, sums, 0.0) + carry_ref[:1, :]
    carry_ref[...] = jnp.broadcast_to(c[tm - 1:tm, :], carry_ref.shape)
    cc_ref[0] = _pack_parts(c)
    stats_ref[0] = c + _head_sq_norms(qf.astype(BF16), kf)


def _in_proj(h, g, wu, wf, ws, fb, *, tm, n_heads):
    b, lp, d = h.shape
    width = ws.shape[1] // 3
    row3 = lambda w: pl.BlockSpec((1, tm, w), lambda bi, i: (bi, i, 0))
    col3 = lambda w: pl.BlockSpec((1, w, tm), lambda bi, i: (bi, 0, i))
    full = lambda arr: pl.BlockSpec(arr.shape, lambda bi, i: (0,) * arr.ndim)
    rows = jax.ShapeDtypeStruct((b, lp, width), BF16)
    cols = jax.ShapeDtypeStruct((b, width, lp), BF16)
    return pl.pallas_call(
        functools.partial(_in_proj_kernel, tm=tm, width=width, n_heads=n_heads),
        grid=(b, lp // tm),
        in_specs=[row3(d), full(g), full(wu), full(wf), full(ws), full(fb)],
        out_specs=[row3(wu.shape[1]), col3(width), row3(width), col3(width), row3(LANES), row3(LANES),
                   col3(width), row3(width), col3(width)],
        out_shape=[jax.ShapeDtypeStruct((b, lp, wu.shape[1]), F32), cols, rows, cols,
                   jax.ShapeDtypeStruct((b, lp, LANES), BF16),
                   jax.ShapeDtypeStruct((b, lp, LANES), F32), cols, rows, cols],
        scratch_shapes=[pltpu.VMEM((8, LANES), F32)],
        compiler_params=pltpu.CompilerParams(
            dimension_semantics=("parallel", "arbitrary"), vmem_limit_bytes=VMEM_LIMIT),
        name="in_proj",
    )(h, g, wu, wf, ws, fb)


def _fox_plan_kernel(stats_ref, js_ref, *, tq, tk):
    s = stats_ref[0]
    lp = s.shape[0]
    rows_out = js_ref.shape[1]
    lane = lax.broadcasted_iota(jnp.int32, s.shape, 1)
    k_max_sq = pltpu.roll(jnp.max(s, axis=0, keepdims=True), LANES - 2 * PART_STRIDE, 1)
    q_sq = pltpu.roll(s, LANES - PART_STRIDE, 1)
    spread = jnp.where(lane < PART_STRIDE, jnp.sqrt(q_sq * k_max_sq), 0.0)
    reach = s + (2.0 * NORM_SLACK) * spread + ZERO_EXP2
    tiles = [jnp.max(reach[i * tq:(i + 1) * tq], axis=0, keepdims=True) for i in range(lp // tq)]
    tiles.append(jnp.full((rows_out - len(tiles), LANES), -NEG_BIG, F32))
    reach_max = jnp.concatenate(tiles, axis=0)
    n_key_tiles = lp // tk
    c_end = stats_ref[0, pl.ds(tk - 1, n_key_tiles, stride=tk), :]
    count = jnp.zeros(reach_max.shape, jnp.int32)
    for j in range(n_key_tiles):
        count = count + (c_end[j:j + 1, :] > reach_max).astype(jnp.int32)
    js_ref[0] = jnp.minimum(count, pltpu.roll(count, LANES - 1, 1))


def _fox_plan(stats, *, tq, tk):
    b, lp, _ = stats.shape
    rows_out = -(-(lp // tq) // 8) * 8
    return pl.pallas_call(
        functools.partial(_fox_plan_kernel, tq=tq, tk=tk),
        grid=(b,),
        in_specs=[pl.BlockSpec((1, lp, LANES), lambda bi: (bi, 0, 0))],
        out_specs=pl.BlockSpec((1, rows_out, LANES), lambda bi: (bi, 0, 0)),
        out_shape=jax.ShapeDtypeStruct((b, rows_out, LANES), jnp.int32),
        compiler_params=pltpu.CompilerParams(
            dimension_semantics=("parallel",), vmem_limit_bytes=VMEM_LIMIT),
        name="fox_plan",
    )(stats)


def _head_norm_t(o_t):
    ms = jnp.mean(o_t * o_t, axis=0, keepdims=True)
    return o_t * lax.rsqrt(ms + EPS)


def _fox_kernel(first_ref, qt_ref, k_ref, vt_ref, cc_ref, g_ref, o_ref,
                qa_ref, m_ref, alpha_ref, pt_ref, st_ref, acc_ref, *, tq, tk):
    p = pl.program_id(1)
    qi = pl.program_id(2)
    nb = tq // tk
    n_full = nb * qi
    step = (pl.program_id(0) * pl.num_programs(1) + p) * pl.num_programs(2) + qi
    first = jnp.minimum(first_ref[step], n_full)

    qt = qt_ref[0]
    r = lax.broadcasted_iota(jnp.int32, qt.shape, 0)
    zero = jnp.zeros_like(qt)
    for hh in (0, 1):
        own = (r < HEAD_DIM) if hh == 0 else (r >= HEAD_DIM)
        h = 2 * p + hh
        picks = (r == h) | (r == h + PART_STRIDE) | (r == h + 2 * PART_STRIDE)
        qa_ref[hh, :LANES, :] = jnp.where(own, qt, zero)
        qa_ref[hh, LANES:, :] = jnp.where(picks, -1.0, 0.0).astype(BF16)
    m_ref[...] = jnp.full(m_ref.shape, NEG_BIG, F32)
    acc_ref[...] = jnp.zeros(acc_ref.shape, F32)
    ones = jnp.ones((BF16_ROWS, tk), BF16)

    def pieces_from(c0):
        return [(hh, c) for hh in (0, 1) for c in range(c0, tq, tk)]

    def scores(j, pieces):
        k0 = pl.multiple_of(j * tk, tk)
        kk = jnp.concatenate([k_ref[0, pl.ds(k0, tk), :], cc_ref[0, pl.ds(k0, tk), :]], axis=1)
        return [_dot(kk, qa_ref[hh, :, c:c + tk]) for hh, c in pieces]

    def softmax(st, pieces, masked):
        if masked:
            row = lax.broadcasted_iota(jnp.int32, (tk, tk), 0)
            col = lax.broadcasted_iota(jnp.int32, (tk, tk), 1)
            c0 = pieces[0][1]
            st = [jnp.where(row <= col, s, NEG_BIG) if c == c0 else s for s, (_, c) in zip(st, pieces)]
        weights = []
        for s, (hh, c) in zip(st, pieces):
            m_old = m_ref[hh, :, c:c + tk]
            m_new = jnp.maximum(m_old, jnp.max(s, axis=0, keepdims=True))
            m_ref[hh, :, c:c + tk] = m_new
            weights.append((jnp.exp2(s - m_new).astype(BF16), jnp.exp2(m_old - m_new)))
        return weights

    def accumulate(j, pieces, weights):
        k0 = pl.multiple_of(j * tk, tk)
        for (hh, c), (pt, alpha) in zip(pieces, weights):
            vt = jnp.concatenate(
                [vt_ref[0, hh * HEAD_DIM:(hh + 1) * HEAD_DIM, pl.ds(k0, tk)], ones], axis=0)
            acc_ref[hh, :, c:c + tk] = alpha * acc_ref[hh, :, c:c + tk] + _dot(vt, pt)

    every = pieces_from(0)
    alpha_ref[...] = jnp.ones(alpha_ref.shape, F32)
    pt_ref[...] = jnp.zeros(pt_ref.shape, BF16)

    def put_scores(st):
        for s, (hh, c) in zip(st, every):
            st_ref[hh, :, c:c + tk] = s

    def get_scores():
        return [st_ref[hh, :, c:c + tk] for hh, c in every]

    def put_weights(weights):
        for (pt, alpha), (hh, c) in zip(weights, every):
            pt_ref[hh, :, c:c + tk] = pt
            alpha_ref[hh, :, c:c + tk] = alpha

    def get_weights():
        return [(pt_ref[hh, :, c:c + tk], alpha_ref[hh, :, c:c + tk]) for hh, c in every]

    def full_tile(j, carry):
        ahead = scores(j + 1, every)
        accumulate(jnp.maximum(j - 1, 0), every, get_weights())
        put_weights(softmax(get_scores(), every, False))
        put_scores(ahead)
        return carry

    put_scores(scores(first, every))
    lax.fori_loop(first, n_full, full_tile, 0)
    diagonal = [pieces_from(d * tk) for d in range(nb)]
    st = [get_scores()] + [scores(n_full + d, diagonal[d]) for d in range(1, nb)]
    accumulate(jnp.maximum(n_full - 1, 0), every, get_weights())
    weights = [softmax(st[d], diagonal[d], True) for d in range(nb)]
    for d in range(nb):
        accumulate(n_full + d, diagonal[d], weights[d])

    out = []
    for hh in (0, 1):
        acc = acc_ref[hh]
        out.append(_head_norm_t(acc[:HEAD_DIM] / acc[HEAD_DIM:HEAD_DIM + 1]))
    o_ref[0] = (jnp.concatenate(out, axis=0).T * g_ref[...]).astype(o_ref.dtype)


def _fox_attention(first, qt, k, vt, cc, gain, *, tq, tk):
    b, lp, width = k.shape
    pairs = width // LANES
    grid_spec = pltpu.PrefetchScalarGridSpec(
        num_scalar_prefetch=1,
        grid=(b, pairs, lp // tq),
        in_specs=[pl.BlockSpec((1, LANES, tq), lambda bi, p, qi, first: (bi, p, qi)),
                  pl.BlockSpec((1, lp, LANES), lambda bi, p, qi, first: (bi, 0, p)),
                  pl.BlockSpec((1, LANES, lp), lambda bi, p, qi, first: (bi, p, 0)),
                  pl.BlockSpec((1, lp, LANES), lambda bi, p, qi, first: (bi, 0, 0)),
                  pl.BlockSpec((1, LANES), lambda bi, p, qi, first: (0, p))],
        out_specs=pl.BlockSpec((1, tq, LANES), lambda bi, p, qi, first: (bi, qi, p)),
        scratch_shapes=[pltpu.VMEM((2, 2 * LANES, tq), BF16),
                        pltpu.VMEM((2, 1, tq), F32),
                        pltpu.VMEM((2, 1, tq), F32),
                        pltpu.VMEM((2, tk, tq), BF16),
                        pltpu.VMEM((2, tk, tq), F32),
                        pltpu.VMEM((2, HEAD_DIM + BF16_ROWS, tq), F32)])
    return pl.pallas_call(
        functools.partial(_fox_kernel, tq=tq, tk=tk),
        grid_spec=grid_spec,
        out_shape=jax.ShapeDtypeStruct((b, lp, width), BF16),
        compiler_params=pltpu.CompilerParams(
            dimension_semantics=("parallel", "parallel", "parallel"), vmem_limit_bytes=VMEM_LIMIT),
        name="fox_attention",
    )(first, qt, k, vt, cc, gain)


def _sb_kernel(qt_ref, k_ref, vt_ref, tri_ref, g_ref, o_ref, qa_ref, tail_ref, acc_ref, *, t, n_heads):
    qi = pl.program_id(1)
    for h in range(n_heads):
        qt = qt_ref[0, h * HEAD_DIM:(h + 1) * HEAD_DIM, :]
        zero = jnp.zeros_like(qt)
        qa_ref[h] = jnp.concatenate([qt, zero] if h % 2 == 0 else [zero, qt], axis=0)
    tail_ref[...] = jnp.zeros(tail_ref.shape, F32)
    acc_ref[...] = jnp.zeros(acc_ref.shape, F32)

    def key_tile(j, masked):
        k0 = pl.multiple_of(j * t, t)
        tri = tri_ref[...]
        heads = range(n_heads)
        if masked:
            row = lax.broadcasted_iota(jnp.int32, (t, t), 0)
            col = lax.broadcasted_iota(jnp.int32, (t, t), 1)
            strict = row < col
        z = [_dot(k_ref[0, pl.ds(k0, t), (h // 2) * LANES:(h // 2 + 1) * LANES], qa_ref[h])
             for h in heads]
        split = []
        for h in heads:
            sp = jnp.maximum(z[h], 0.0) + jnp.log2(1.0 + jnp.exp2(-jnp.abs(z[h])))
            if masked:
                sp = jnp.where(strict, sp, 0.0)
            hi = sp.astype(BF16)
            split.append((hi, (sp - hi.astype(F32)).astype(BF16)))
        cum = [_dot(tri, jnp.concatenate([hi, lo], axis=0)) for hi, lo in split]
        weights = []
        for h in heads:
            tail = tail_ref[h]
            e = z[h] - cum[h] + tail
            if masked:
                e = jnp.where(strict, e, NEG_BIG)
            weights.append(jnp.exp2(e).astype(BF16))
            tail_ref[h] = tail - cum[h][:1, :]
        for h in heads:
            vt = vt_ref[0, h * HEAD_DIM:(h + 1) * HEAD_DIM, pl.ds(k0, t)]
            acc_ref[h] += _dot(vt, weights[h])

    def live():
        return jnp.max(tail_ref[...]) > -ZERO_EXP2

    key_tile(qi, True)

    def cond(carry):
        j, go = carry
        return jnp.logical_and(j >= 0, go)

    def body(carry):
        j, _ = carry
        key_tile(j, False)
        return j - 1, live()

    lax.while_loop(cond, body, (qi - 1, live()))
    out = jnp.concatenate([_head_norm_t(acc_ref[h]) for h in range(n_heads)], axis=0)
    o_ref[0] = (out.T * g_ref[...]).astype(o_ref.dtype)


def _sb_attention(qt, k, vt, gain, *, t):
    b, lp, width = k.shape
    n_heads = width // HEAD_DIM
    idx = jnp.arange(t)
    tri = (idx[None, :] >= idx[:, None]).astype(BF16)
    tri = jnp.concatenate([tri, tri], axis=1)
    return pl.pallas_call(
        functools.partial(_sb_kernel, t=t, n_heads=n_heads),
        grid=(b, lp // t),
        in_specs=[pl.BlockSpec((1, width, t), lambda bi, qi: (bi, 0, qi)),
                  pl.BlockSpec((1, lp, width), lambda bi, qi: (bi, 0, 0)),
                  pl.BlockSpec((1, width, lp), lambda bi, qi: (bi, 0, 0)),
                  pl.BlockSpec((t, 2 * t), lambda bi, qi: (0, 0)),
                  pl.BlockSpec((1, width), lambda bi, qi: (0, 0))],
        out_specs=pl.BlockSpec((1, t, width), lambda bi, qi: (bi, qi, 0)),
        out_shape=jax.ShapeDtypeStruct((b, lp, width), BF16),
        scratch_shapes=[pltpu.VMEM((n_heads, LANES, t), BF16),
                        pltpu.VMEM((n_heads, 1, t), F32),
                        pltpu.VMEM((n_heads, HEAD_DIM, t), F32)],
        compiler_params=pltpu.CompilerParams(
            dimension_semantics=("parallel", "parallel"), vmem_limit_bytes=VMEM_LIMIT),
        name="sb_attention",
    )(qt, k, vt, tri, gain)


def _out_proj_kernel(h_ref, u_ref, up_ref, yf_ref, ys_ref, pw_ref, psc_ref, wo_ref, o_ref, *, tm):
    i = pl.program_id(1)
    u = u_ref[0]
    halo = jnp.where(i == 0, 0.0, up_ref[0])
    x = jnp.concatenate([halo, u], axis=0)
    sums = []
    shift = 1
    for _ in POOL_WINDOWS:
        x = x + pltpu.roll(x, shift, 0)
        sums.append(x[HALO:])
        shift *= 2
    group = lax.broadcasted_iota(jnp.int32, u.shape, 1) // POOL_GROUP
    window_sum = sums[-1]
    window = jnp.full(u.shape, POOL_WINDOWS[-1], jnp.int32)
    for g in range(len(POOL_WINDOWS) - 2, -1, -1):
        window_sum = jnp.where(group == g, sums[g], window_sum)
        window = jnp.where(group == g, POOL_WINDOWS[g], window)
    t1 = i * tm + lax.broadcasted_iota(jnp.int32, u.shape, 0) + 1
    count = jnp.minimum(t1, window).astype(F32)
    d = (window_sum / count - u).astype(BF16)
    y_pool = (_dot(d, pw_ref[...]) * psc_ref[...]).astype(BF16)
    y = jnp.concatenate([y_pool, yf_ref[0], ys_ref[0]], axis=1)
    o_ref[0] = h_ref[0] + _dot(y, wo_ref[...])


def _out_proj(h, u, yf, ys, pw, psc, wo, *, tm):
    b, lp, d = h.shape
    row3 = lambda w: pl.BlockSpec((1, tm, w), lambda bi, i: (bi, i, 0))
    full = lambda arr: pl.BlockSpec(arr.shape, lambda bi, i: (0,) * arr.ndim)
    halo_spec = pl.BlockSpec((1, HALO, u.shape[2]),
                             lambda bi, i: (bi, jnp.maximum(i * (tm // HALO) - 1, 0), 0))
    return pl.pallas_call(
        functools.partial(_out_proj_kernel, tm=tm),
        grid=(b, lp // tm),
        in_specs=[row3(d), row3(u.shape[2]), halo_spec, row3(yf.shape[2]), row3(ys.shape[2]),
                  full(pw), full(psc), full(wo)],
        out_specs=row3(d),
        out_shape=jax.ShapeDtypeStruct((b, lp, d), F32),
        compiler_params=pltpu.CompilerParams(
            dimension_semantics=("parallel", "parallel"), vmem_limit_bytes=VMEM_LIMIT),
        name="out_proj",
    )(h, u, u, yf, ys, pw, psc, wo)


def _ffn_kernel(h_ref, g_ref, wg_ref, wu_ref, wd_ref, fg_ref, o_ref, *, chunks, final_norm):
    h = h_ref[0]
    a = _rms(h, g_ref[...]).astype(BF16)
    cw = wg_ref.shape[1] // chunks
    out = h
    for c in range(chunks):
        gate = _dot(a, wg_ref[:, c * cw:(c + 1) * cw])
        up = _dot(a, wu_ref[:, c * cw:(c + 1) * cw])
        act = (gate * jax.nn.sigmoid(gate) * up).astype(BF16)
        out = out + _dot(act, wd_ref[c * cw:(c + 1) * cw, :])
    if final_norm:
        out = _rms(out, fg_ref[...])
    o_ref[0] = out


def _ffn(h, g, wg, wu, wd, fg, *, tm, first_row, n_rows, final_norm):
    b, _, d = h.shape
    rows_in = pl.BlockSpec((pl.Element(1), pl.Element(tm), pl.Element(d)),
                           lambda bi, i: (bi, pl.multiple_of(first_row + i * tm, 8), 0))
    rows_out = pl.BlockSpec((1, tm, d), lambda bi, i: (bi, i, 0))
    resident = lambda arr: pl.BlockSpec(arr.shape, lambda bi, i: (0,) * arr.ndim,
                                        pipeline_mode=pl.Buffered(1))
    return pl.pallas_call(
        functools.partial(_ffn_kernel, chunks=FF_CHUNKS, final_norm=final_norm),
        grid=(b, n_rows // tm),
        in_specs=[rows_in, resident(g), resident(wg), resident(wu), resident(wd), resident(fg)],
        out_specs=rows_out,
        out_shape=jax.ShapeDtypeStruct((b, n_rows, d), F32),
        compiler_params=pltpu.CompilerParams(
            dimension_semantics=("parallel", "parallel"), vmem_limit_bytes=VMEM_LIMIT),
        name="ffn",
    )(h, g, wg, wu, wd, fg)


def _largest_tile(n, limit):
    return max(t for t in range(8, limit + 1, 8) if n % t == 0)


def _block_diag(pool_w):
    groups, cin, cout = pool_w.shape
    out = jnp.zeros((groups * cin, groups * cout), pool_w.dtype)
    for g in range(groups):
        out = out.at[g * cin:(g + 1) * cin, g * cout:(g + 1) * cout].set(pool_w[g])
    return out


def _trunk(x, meta_tokens, norm1, w_in, forget_bias, pool_w, pool_scale, fox_out_gain,
           sb_out_gain, w_out, norm2, w_gate, w_up, w_down, final_norm, *,
           attn_tile, fox_q_blocks, row_tile):
    b, s_len, d = x.shape
    depth = norm1.shape[0]
    l = N_META + s_len
    fox_tq = fox_q_blocks * attn_tile
    step = fox_tq * row_tile // math.gcd(fox_tq, row_tile)
    lp = -(-l // step) * step
    pool_width = pool_scale.shape[1]
    fox_w = fox_out_gain.shape[1]
    fox_heads = fox_w // HEAD_DIM
    assert fox_heads <= PART_STRIDE
    meta = jnp.broadcast_to(meta_tokens[None].astype(x.dtype), (b, N_META, d))
    h = jnp.concatenate([meta, x, jnp.zeros((b, lp - l, d), x.dtype)], axis=1)

    o0 = pool_width
    o1 = o0 + 3 * fox_w
    o2 = o1 + fox_heads
    for i in range(depth):
        w_fox = jnp.pad(w_in[i, :, o0:o2].astype(BF16), ((0, 0), (0, LANES - fox_heads)))
        fb = jnp.zeros((1, LANES), F32).at[0, :fox_heads].set(forget_bias[i].astype(F32))
        u, qft, kf, vft, cc, stats, qst, ks, vst = _in_proj(
            h, norm1[i][None], w_in[i, :, :o0].astype(BF16), w_fox, w_in[i, :, o2:].astype(BF16), fb,
            tm=row_tile, n_heads=fox_heads)
        plan = _fox_plan(stats, tq=fox_tq, tk=attn_tile)
        first = plan[:, :lp // fox_tq, 0:fox_heads:2].transpose(0, 2, 1).reshape(-1)
        y_fox = _fox_attention(first, qft, kf, vft, cc, fox_out_gain[i][None], tq=fox_tq, tk=attn_tile)
        y_sb = _sb_attention(qst, ks, vst, sb_out_gain[i][None], t=attn_tile)
        h = _out_proj(h, u, y_fox, y_sb, _block_diag(pool_w[i]).astype(BF16), pool_scale[i][None],
                      w_out[i].astype(BF16), tm=row_tile)
        last = i == depth - 1
        rows = dict(tm=_largest_tile(s_len, FFN_LAST_TILE), first_row=N_META, n_rows=s_len) if last \
            else dict(tm=row_tile, first_row=0, n_rows=lp)
        h = _ffn(h, norm2[i][None], w_gate[i].astype(BF16), w_up[i].astype(BF16),
                 w_down[i].astype(BF16), final_norm[None], final_norm=last, **rows)
    return h


def kernel(x, meta_tokens, norm1, w_in, forget_bias, pool_w, pool_scale, fox_out_gain, sb_out_gain,
           w_out, norm2, w_gate, w_up, w_down, final_norm):
    return _trunk(x, meta_tokens, norm1, w_in, forget_bias, pool_w, pool_scale, fox_out_gain,
                  sb_out_gain, w_out, norm2, w_gate, w_up, w_down, final_norm,
                  attn_tile=ATTN_TILE, fox_q_blocks=FOX_Q_BLOCKS, row_tile=ROW_TILE)
```

```python
import functools
import math

import jax
import jax.numpy as jnp
from jax import lax
from jax.experimental import pallas as pl
from jax.experimental.pallas import tpu as pltpu

HEAD_DIM = 64
N_META = 16
EPS = 1e-6
POOL_WINDOWS = (2, 4, 8, 16)
POOL_GROUP = 64
LANES = 128
BF16_ROWS = 16
HALO = 16
PART_STRIDE = 8
LOG2E = math.log2(math.e)
NEG_BIG = -1e30
ZERO_EXP2 = 160.0
NORM_SLACK = 1.02
VMEM_LIMIT = 56 * 1024 * 1024
F32 = jnp.float32
BF16 = jnp.bfloat16

ATTN_TILE = 256
FOX_Q_BLOCKS = 3
ROW_TILE = 768
FFN_LAST_TILE = 512
FF_CHUNKS = 11
SB_SUM_PARTS = 1


def _dot(a, b):
    return jnp.dot(a, b, preferred_element_type=F32)


def _rms(x, gain):
    ms = jnp.mean(x * x, axis=-1, keepdims=True)
    return x * lax.rsqrt(ms + EPS) * gain


def _split3(x):
    hi = x.astype(BF16)
    r = x - hi.astype(F32)
    mid = r.astype(BF16)
    lo = (r - mid.astype(F32)).astype(BF16)
    return hi, mid, lo


def _pack_parts(x):
    hi, mid, lo = _split3(x)
    return (hi.astype(F32) + pltpu.roll(mid.astype(F32), PART_STRIDE, 1)
            + pltpu.roll(lo.astype(F32), 2 * PART_STRIDE, 1)).astype(BF16)


def _head_sq_norms(q, k):
    width = q.shape[1]
    sq = jnp.concatenate([(x.astype(F32) * x.astype(F32)).astype(BF16) for x in (q, k)], axis=1)
    d = lax.broadcasted_iota(jnp.int32, (2 * width, LANES), 0)
    lane = lax.broadcasted_iota(jnp.int32, (2 * width, LANES), 1)
    target = jnp.where(d < width, PART_STRIDE + d // HEAD_DIM, 2 * PART_STRIDE + (d - width) // HEAD_DIM)
    return _dot(sq, (lane == target).astype(BF16))


def _in_proj_kernel(h_ref, g_ref, wu_ref, wf_ref, ws_ref, fb_ref,
                    u_ref, qf_ref, kf_ref, vf_ref, cc_ref, stats_ref, qs_ref, ks_ref, vs_ref,
                    carry_ref, *, tm, width, n_heads):
    i = pl.program_id(1)

    @pl.when(i == 0)
    def _():
        carry_ref[...] = jnp.zeros_like(carry_ref)

    a = _rms(h_ref[0], g_ref[...]).astype(BF16)
    u_ref[0] = _dot(a, wu_ref[...])

    scale = HEAD_DIM ** -0.5
    pf = _dot(a, wf_ref[...])
    qf = pf[:, :width] * (scale * LOG2E)
    kf = pf[:, width:2 * width].astype(BF16)
    qf_ref[0] = qf.T.astype(BF16)
    kf_ref[0] = kf
    vf_ref[0] = pf[:, 2 * width:3 * width].T.astype(BF16)
    ps = _dot(a, ws_ref[...])
    qs_ref[0] = (ps[:, :width] * (scale * LOG2E)).T.astype(BF16)
    ks_ref[0] = ps[:, width:2 * width].astype(BF16)
    vs_ref[0] = ps[:, 2 * width:].T.astype(BF16)

    fl = pf[:, 3 * width:] + fb_ref[...]
    lane = lax.broadcasted_iota(jnp.int32, fl.shape, 1)
    log_f = jnp.where(lane < n_heads,
                      (jnp.minimum(fl, 0.0) - jnp.log1p(jnp.exp(-jnp.abs(fl)))) * LOG2E, 0.0)
    row = lax.broadcasted_iota(jnp.int32, (tm, tm), 0)
    col = lax.broadcasted_iota(jnp.int32, (tm, tm), 1)
    sums = _dot((col <= row).astype(BF16), _pack_parts(log_f))
    sums = sums + pltpu.roll(sums, LANES - PART_STRIDE, 1) + pltpu.roll(sums, LANES - 2 * PART_STRIDE, 1)
    c = jnp.where(lane < PART_STRIDE, sums, 0.0) + carry_ref[:1, :]
    carry_ref[...] = jnp.broadcast_to(c[tm - 1:tm, :], carry_ref.shape)
    cc_ref[0] = _pack_parts(c)
    stats_ref[0] = c + _head_sq_norms(qf.astype(BF16), kf)


def _in_proj(h, g, wu, wf, ws, fb, *, tm, n_heads):
    b, lp, d = h.shape
    width = ws.shape[1] // 3
    row3 = lambda w: pl.BlockSpec((1, tm, w), lambda bi, i: (bi, i, 0))
    col3 = lambda w: pl.BlockSpec((1, w, tm), lambda bi, i: (bi, 0, i))
    full = lambda arr: pl.BlockSpec(arr.shape, lambda bi, i: (0,) * arr.ndim)
    rows = jax.ShapeDtypeStruct((b, lp, width), BF16)
    cols = jax.ShapeDtypeStruct((b, width, lp), BF16)
    return pl.pallas_call(
        functools.partial(_in_proj_kernel, tm=tm, width=width, n_heads=n_heads),
        grid=(b, lp // tm),
        in_specs=[row3(d), full(g), full(wu), full(wf), full(ws), full(fb)],
        out_specs=[row3(wu.shape[1]), col3(width), row3(width), col3(width), row3(LANES), row3(LANES),
                   col3(width), row3(width), col3(width)],
        out_shape=[jax.ShapeDtypeStruct((b, lp, wu.shape[1]), F32), cols, rows, cols,
                   jax.ShapeDtypeStruct((b, lp, LANES), BF16),
                   jax.ShapeDtypeStruct((b, lp, LANES), F32), cols, rows, cols],
        scratch_shapes=[pltpu.VMEM((8, LANES), F32)],
        compiler_params=pltpu.CompilerParams(
            dimension_semantics=("parallel", "arbitrary"), vmem_limit_bytes=VMEM_LIMIT),
        name="in_proj",
    )(h, g, wu, wf, ws, fb)


def _fox_plan_kernel(stats_ref, js_ref, *, tq, tk):
    s = stats_ref[0]
    lp = s.shape[0]
    rows_out = js_ref.shape[1]
    lane = lax.broadcasted_iota(jnp.int32, s.shape, 1)
    k_max_sq = pltpu.roll(jnp.max(s, axis=0, keepdims=True), LANES - 2 * PART_STRIDE, 1)
    q_sq = pltpu.roll(s, LANES - PART_STRIDE, 1)
    spread = jnp.where(lane < PART_STRIDE, jnp.sqrt(q_sq * k_max_sq), 0.0)
    reach = s + (2.0 * NORM_SLACK) * spread + ZERO_EXP2
    tiles = [jnp.max(reach[i * tq:(i + 1) * tq], axis=0, keepdims=True) for i in range(lp // tq)]
    tiles.append(jnp.full((rows_out - len(tiles), LANES), -NEG_BIG, F32))
    reach_max = jnp.concatenate(tiles, axis=0)
    n_key_tiles = lp // tk
    c_end = stats_ref[0, pl.ds(tk - 1, n_key_tiles, stride=tk), :]
    count = jnp.zeros(reach_max.shape, jnp.int32)
    for j in range(n_key_tiles):
        count = count + (c_end[j:j + 1, :] > reach_max).astype(jnp.int32)
    js_ref[0] = jnp.minimum(count, pltpu.roll(count, LANES - 1, 1))


def _fox_plan(stats, *, tq, tk):
    b, lp, _ = stats.shape
    rows_out = -(-(lp // tq) // 8) * 8
    return pl.pallas_call(
        functools.partial(_fox_plan_kernel, tq=tq, tk=tk),
        grid=(b,),
        in_specs=[pl.BlockSpec((1, lp, LANES), lambda bi: (bi, 0, 0))],
        out_specs=pl.BlockSpec((1, rows_out, LANES), lambda bi: (bi, 0, 0)),
        out_shape=jax.ShapeDtypeStruct((b, rows_out, LANES), jnp.int32),
        compiler_params=pltpu.CompilerParams(
            dimension_semantics=("parallel",), vmem_limit_bytes=VMEM_LIMIT),
        name="fox_plan",
    )(stats)


def _head_norm_t(o_t):
    ms = jnp.mean(o_t * o_t, axis=0, keepdims=True)
    return o_t * lax.rsqrt(ms + EPS)


def _fox_kernel(first_ref, qt_ref, k_ref, vt_ref, cc_ref, g_ref, o_ref,
                qa_ref, m_ref, alpha_ref, pt_ref, st_ref, acc_ref, *, tq, tk):
    p = pl.program_id(1)
    qi = pl.program_id(2)
    nb = tq // tk
    n_full = nb * qi
    step = (pl.program_id(0) * pl.num_programs(1) + p) * pl.num_programs(2) + qi
    first = jnp.minimum(first_ref[step], n_full)

    qt = qt_ref[0]
    r = lax.broadcasted_iota(jnp.int32, qt.shape, 0)
    zero = jnp.zeros_like(qt)
    for hh in (0, 1):
        own = (r < HEAD_DIM) if hh == 0 else (r >= HEAD_DIM)
        h = 2 * p + hh
        picks = (r == h) | (r == h + PART_STRIDE) | (r == h + 2 * PART_STRIDE)
        qa_ref[hh, :LANES, :] = jnp.where(own, qt, zero)
        qa_ref[hh, LANES:, :] = jnp.where(picks, -1.0, 0.0).astype(BF16)
    m_ref[...] = jnp.full(m_ref.shape, NEG_BIG, F32)
    acc_ref[...] = jnp.zeros(acc_ref.shape, F32)
    ones = jnp.ones((BF16_ROWS, tk), BF16)

    def pieces_from(c0):
        return [(hh, c) for hh in (0, 1) for c in range(c0, tq, tk)]

    def scores(j, pieces):
        k0 = pl.multiple_of(j * tk, tk)
        kk = jnp.concatenate([k_ref[0, pl.ds(k0, tk), :], cc_ref[0, pl.ds(k0, tk), :]], axis=1)
        return [_dot(kk, qa_ref[hh, :, c:c + tk]) for hh, c in pieces]

    def softmax(st, pieces, masked):
        if masked:
            row = lax.broadcasted_iota(jnp.int32, (tk, tk), 0)
            col = lax.broadcasted_iota(jnp.int32, (tk, tk), 1)
            c0 = pieces[0][1]
            st = [jnp.where(row <= col, s, NEG_BIG) if c == c0 else s for s, (_, c) in zip(st, pieces)]
        weights = []
        for s, (hh, c) in zip(st, pieces):
            m_old = m_ref[hh, :, c:c + tk]
            m_new = jnp.maximum(m_old, jnp.max(s, axis=0, keepdims=True))
            m_ref[hh, :, c:c + tk] = m_new
            weights.append((jnp.exp2(s - m_new).astype(BF16), jnp.exp2(m_old - m_new)))
        return weights

    def accumulate(j, pieces, weights):
        k0 = pl.multiple_of(j * tk, tk)
        for (hh, c), (pt, alpha) in zip(pieces, weights):
            vt = jnp.concatenate(
                [vt_ref[0, hh * HEAD_DIM:(hh + 1) * HEAD_DIM, pl.ds(k0, tk)], ones], axis=0)
            acc_ref[hh, :, c:c + tk] = alpha * acc_ref[hh, :, c:c + tk] + _dot(vt, pt)

    every = pieces_from(0)
    alpha_ref[...] = jnp.ones(alpha_ref.shape, F32)
    pt_ref[...] = jnp.zeros(pt_ref.shape, BF16)

    def put_scores(st):
        for s, (hh, c) in zip(st, every):
            st_ref[hh, :, c:c + tk] = s

    def get_scores():
        return [st_ref[hh, :, c:c + tk] for hh, c in every]

    def put_weights(weights):
        for (pt, alpha), (hh, c) in zip(weights, every):
            pt_ref[hh, :, c:c + tk] = pt
            alpha_ref[hh, :, c:c + tk] = alpha

    def get_weights():
        return [(pt_ref[hh, :, c:c + tk], alpha_ref[hh, :, c:c + tk]) for hh, c in every]

    def full_tile(j, carry):
        ahead = scores(j + 1, every)
        accumulate(jnp.maximum(j - 1, 0), every, get_weights())
        put_weights(softmax(get_scores(), every, False))
        put_scores(ahead)
        return carry

    put_scores(scores(first, every))
    lax.fori_loop(first, n_full, full_tile, 0)
    diagonal = [pieces_from(d * tk) for d in range(nb)]
    st = [get_scores()] + [scores(n_full + d, diagonal[d]) for d in range(1, nb)]
    accumulate(jnp.maximum(n_full - 1, 0), every, get_weights())
    weights = [softmax(st[d], diagonal[d], True) for d in range(nb)]
    for d in range(nb):
        accumulate(n_full + d, diagonal[d], weights[d])

    out = []
    for hh in (0, 1):
        acc = acc_ref[hh]
        out.append(_head_norm_t(acc[:HEAD_DIM] / acc[HEAD_DIM:HEAD_DIM + 1]))
    o_ref[0] = (jnp.concatenate(out, axis=0).T * g_ref[...]).astype(o_ref.dtype)


def _fox_attention(first, qt, k, vt, cc, gain, *, tq, tk):
    b, lp, width = k.shape
    pairs = width // LANES
    grid_spec = pltpu.PrefetchScalarGridSpec(
        num_scalar_prefetch=1,
        grid=(b, pairs, lp // tq),
        in_specs=[pl.BlockSpec((1, LANES, tq), lambda bi, p, qi, first: (bi, p, qi)),
                  pl.BlockSpec((1, lp, LANES), lambda bi, p, qi, first: (bi, 0, p)),
                  pl.BlockSpec((1, LANES, lp), lambda bi, p, qi, first: (bi, p, 0)),
                  pl.BlockSpec((1, lp, LANES), lambda bi, p, qi, first: (bi, 0, 0)),
                  pl.BlockSpec((1, LANES), lambda bi, p, qi, first: (0, p))],
        out_specs=pl.BlockSpec((1, tq, LANES), lambda bi, p, qi, first: (bi, qi, p)),
        scratch_shapes=[pltpu.VMEM((2, 2 * LANES, tq), BF16),
                        pltpu.VMEM((2, 1, tq), F32),
                        pltpu.VMEM((2, 1, tq), F32),
                        pltpu.VMEM((2, tk, tq), BF16),
                        pltpu.VMEM((2, tk, tq), F32),
                        pltpu.VMEM((2, HEAD_DIM + BF16_ROWS, tq), F32)])
    return pl.pallas_call(
        functools.partial(_fox_kernel, tq=tq, tk=tk),
        grid_spec=grid_spec,
        out_shape=jax.ShapeDtypeStruct((b, lp, width), BF16),
        compiler_params=pltpu.CompilerParams(
            dimension_semantics=("parallel", "parallel", "parallel"), vmem_limit_bytes=VMEM_LIMIT),
        name="fox_attention",
    )(first, qt, k, vt, cc, gain)


def _sb_kernel(qt_ref, k_ref, vt_ref, tri_ref, g_ref, o_ref, qa_ref, tail_ref, acc_ref,
               *, t, n_heads, group):
    gi = pl.program_id(1)
    pieces = [(a, h) for a in range(group) for h in range(n_heads)]
    for a, h in pieces:
        qt = qt_ref[0, h * HEAD_DIM:(h + 1) * HEAD_DIM, a * t:(a + 1) * t]
        zero = jnp.zeros_like(qt)
        qa_ref[a, h] = jnp.concatenate([qt, zero] if h % 2 == 0 else [zero, qt], axis=0)

    def key_tiles(back, masked):
        tri = tri_ref[...]
        own = [gi * group + a for a in range(group)]
        exists = [own[a] >= back for a in range(group)]
        k0 = [pl.multiple_of(jnp.maximum(own[a] - back, 0) * t, t) for a in range(group)]
        if masked:
            row = lax.broadcasted_iota(jnp.int32, (t, t), 0)
            col = lax.broadcasted_iota(jnp.int32, (t, t), 1)
            strict = row < col
        z, split, cum, weights, scale = {}, {}, {}, {}, {}

        def logits(a, h):
            z[a, h] = _dot(k_ref[0, pl.ds(k0[a], t), (h // 2) * LANES:(h // 2 + 1) * LANES], qa_ref[a, h])

        def softplus_parts(a, h):
            sp = jnp.maximum(z[a, h], 0.0) + jnp.log2(1.0 + jnp.exp2(-jnp.abs(z[a, h])))
            if masked:
                sp = jnp.where(strict, sp, 0.0)
            parts = [sp.astype(BF16)]
            for _ in range(SB_SUM_PARTS - 1):
                sp = sp - parts[-1].astype(F32)
                parts.append(sp.astype(BF16))
            split[a, h] = jnp.concatenate(parts, axis=0)

        def suffix_sums(a, h):
            cum[a, h] = _dot(tri, split.pop((a, h)))

        def stick_weights(a, h):
            c = cum.pop((a, h))
            e = z.pop((a, h)) - c
            if masked:
                e = jnp.where(strict, e, NEG_BIG)
            weights[a, h] = jnp.exp2(e).astype(BF16)
            if masked:
                tail_ref[a, h] = -c[:1, :]
            else:
                tail = jnp.where(exists[a], tail_ref[a, h], NEG_BIG)
                scale[a, h] = jnp.exp2(tail)
                tail_ref[a, h] = tail - c[:1, :]

        def accumulate(a, h):
            vt = vt_ref[0, h * HEAD_DIM:(h + 1) * HEAD_DIM, pl.ds(k0[a], t)]
            pv = _dot(vt, weights.pop((a, h)))
            acc_ref[a, h] = pv if masked else acc_ref[a, h] + scale.pop((a, h)) * pv

        stages = (logits, softplus_parts, suffix_sums, stick_weights, accumulate)
        for step in range(group + len(stages) - 1):
            for s in reversed(range(len(stages))):
                a = step - s
                if 0 <= a < group:
                    for h in range(n_heads):
                        stages[s](a, h)

    def live():
        return jnp.max(tail_ref[...]) > -ZERO_EXP2

    key_tiles(0, True)

    def cond(carry):
        back, go = carry
        return jnp.logical_and(back <= gi * group + (group - 1), go)

    def body(carry):
        back, _ = carry
        key_tiles(back, False)
        return back + 1, live()

    lax.while_loop(cond, body, (1, live()))
    for a in range(group):
        out = jnp.concatenate([_head_norm_t(acc_ref[a, h]) for h in range(n_heads)], axis=0)
        o_ref[0, a * t:(a + 1) * t, :] = (out.T * g_ref[...]).astype(o_ref.dtype)


def _sb_attention(qt, k, vt, gain, *, t, group):
    b, lp, width = k.shape
    n_heads = width // HEAD_DIM
    idx = jnp.arange(t)
    tri = (idx[None, :] >= idx[:, None]).astype(BF16)
    tri = jnp.concatenate([tri] * SB_SUM_PARTS, axis=1)
    return pl.pallas_call(
        functools.partial(_sb_kernel, t=t, n_heads=n_heads, group=group),
        grid=(b, lp // (group * t)),
        in_specs=[pl.BlockSpec((1, width, group * t), lambda bi, gi: (bi, 0, gi)),
                  pl.BlockSpec((1, lp, width), lambda bi, gi: (bi, 0, 0)),
                  pl.BlockSpec((1, width, lp), lambda bi, gi: (bi, 0, 0)),
                  pl.BlockSpec((t, SB_SUM_PARTS * t), lambda bi, gi: (0, 0)),
                  pl.BlockSpec((1, width), lambda bi, gi: (0, 0))],
        out_specs=pl.BlockSpec((1, group * t, width), lambda bi, gi: (bi, gi, 0)),
        out_shape=jax.ShapeDtypeStruct((b, lp, width), BF16),
        scratch_shapes=[pltpu.VMEM((group, n_heads, LANES, t), BF16),
                        pltpu.VMEM((group, n_heads, 1, t), F32),
                        pltpu.VMEM((group, n_heads, HEAD_DIM, t), F32)],
        compiler_params=pltpu.CompilerParams(
            dimension_semantics=("parallel", "parallel"), vmem_limit_bytes=VMEM_LIMIT),
        name="sb_attention",
    )(qt, k, vt, tri, gain)


def _out_proj_kernel(h_ref, u_ref, up_ref, yf_ref, ys_ref, pw_ref, psc_ref, wo_ref, o_ref, *, tm):
    i = pl.program_id(1)
    u = u_ref[0]
    halo = jnp.where(i == 0, 0.0, up_ref[0])
    x = jnp.concatenate([halo, u], axis=0)
    sums = []
    shift = 1
    for _ in POOL_WINDOWS:
        x = x + pltpu.roll(x, shift, 0)
        sums.append(x[HALO:])
        shift *= 2
    group = lax.broadcasted_iota(jnp.int32, u.shape, 1) // POOL_GROUP
    window_sum = sums[-1]
    window = jnp.full(u.shape, POOL_WINDOWS[-1], jnp.int32)
    for g in range(len(POOL_WINDOWS) - 2, -1, -1):
        window_sum = jnp.where(group == g, sums[g], window_sum)
        window = jnp.where(group == g, POOL_WINDOWS[g], window)
    t1 = i * tm + lax.broadcasted_iota(jnp.int32, u.shape, 0) + 1
    count = jnp.minimum(t1, window).astype(F32)
    d = (window_sum / count - u).astype(BF16)
    y_pool = (_dot(d, pw_ref[...]) * psc_ref[...]).astype(BF16)
    y = jnp.concatenate([y_pool, yf_ref[0], ys_ref[0]], axis=1)
    o_ref[0] = h_ref[0] + _dot(y, wo_ref[...])


def _out_proj(h, u, yf, ys, pw, psc, wo, *, tm):
    b, lp, d = h.shape
    row3 = lambda w: pl.BlockSpec((1, tm, w), lambda bi, i: (bi, i, 0))
    full = lambda arr: pl.BlockSpec(arr.shape, lambda bi, i: (0,) * arr.ndim)
    halo_spec = pl.BlockSpec((1, HALO, u.shape[2]),
                             lambda bi, i: (bi, jnp.maximum(i * (tm // HALO) - 1, 0), 0))
    return pl.pallas_call(
        functools.partial(_out_proj_kernel, tm=tm),
        grid=(b, lp // tm),
        in_specs=[row3(d), row3(u.shape[2]), halo_spec, row3(yf.shape[2]), row3(ys.shape[2]),
                  full(pw), full(psc), full(wo)],
        out_specs=row3(d),
        out_shape=jax.ShapeDtypeStruct((b, lp, d), F32),
        compiler_params=pltpu.CompilerParams(
            dimension_semantics=("parallel", "parallel"), vmem_limit_bytes=VMEM_LIMIT),
        name="out_proj",
    )(h, u, u, yf, ys, pw, psc, wo)


def _ffn_kernel(h_ref, g_ref, wg_ref, wu_ref, wd_ref, fg_ref, o_ref, *, chunks, final_norm):
    h = h_ref[0]
    a = _rms(h, g_ref[...]).astype(BF16)
    cw = wg_ref.shape[1] // chunks
    out = h
    for c in range(chunks):
        gate = _dot(a, wg_ref[:, c * cw:(c + 1) * cw])
        up = _dot(a, wu_ref[:, c * cw:(c + 1) * cw])
        act = (gate * jax.nn.sigmoid(gate) * up).astype(BF16)
        out = out + _dot(act, wd_ref[c * cw:(c + 1) * cw, :])
    if final_norm:
        out = _rms(out, fg_ref[...])
    o_ref[0] = out


def _ffn(h, g, wg, wu, wd, fg, *, tm, first_row, n_rows, final_norm):
    b, _, d = h.shape
    rows_in = pl.BlockSpec((pl.Element(1), pl.Element(tm), pl.Element(d)),
                           lambda bi, i: (bi, pl.multiple_of(first_row + i * tm, 8), 0))
    rows_out = pl.BlockSpec((1, tm, d), lambda bi, i: (bi, i, 0))
    resident = lambda arr: pl.BlockSpec(arr.shape, lambda bi, i: (0,) * arr.ndim,
                                        pipeline_mode=pl.Buffered(1))
    return pl.pallas_call(
        functools.partial(_ffn_kernel, chunks=FF_CHUNKS, final_norm=final_norm),
        grid=(b, n_rows // tm),
        in_specs=[rows_in, resident(g), resident(wg), resident(wu), resident(wd), resident(fg)],
        out_specs=rows_out,
        out_shape=jax.ShapeDtypeStruct((b, n_rows, d), F32),
        compiler_params=pltpu.CompilerParams(
            dimension_semantics=("parallel", "parallel"), vmem_limit_bytes=VMEM_LIMIT),
        name="ffn",
    )(h, g, wg, wu, wd, fg)


def _largest_tile(n, limit):
    return max(t for t in range(8, limit + 1, 8) if n % t == 0)


def _block_diag(pool_w):
    groups, cin, cout = pool_w.shape
    out = jnp.zeros((groups * cin, groups * cout), pool_w.dtype)
    for g in range(groups):
        out = out.at[g * cin:(g + 1) * cin, g * cout:(g + 1) * cout].set(pool_w[g])
    return out


def _trunk(x, meta_tokens, norm1, w_in, forget_bias, pool_w, pool_scale, fox_out_gain,
           sb_out_gain, w_out, norm2, w_gate, w_up, w_down, final_norm, *,
           attn_tile, fox_q_blocks, row_tile):
    b, s_len, d = x.shape
    depth = norm1.shape[0]
    l = N_META + s_len
    fox_tq = fox_q_blocks * attn_tile
    step = fox_tq * row_tile // math.gcd(fox_tq, row_tile)
    lp = -(-l // step) * step
    pool_width = pool_scale.shape[1]
    fox_w = fox_out_gain.shape[1]
    fox_heads = fox_w // HEAD_DIM
    assert fox_heads <= PART_STRIDE
    meta = jnp.broadcast_to(meta_tokens[None].astype(x.dtype), (b, N_META, d))
    h = jnp.concatenate([meta, x, jnp.zeros((b, lp - l, d), x.dtype)], axis=1)

    o0 = pool_width
    o1 = o0 + 3 * fox_w
    o2 = o1 + fox_heads
    for i in range(depth):
        w_fox = jnp.pad(w_in[i, :, o0:o2].astype(BF16), ((0, 0), (0, LANES - fox_heads)))
        fb = jnp.zeros((1, LANES), F32).at[0, :fox_heads].set(forget_bias[i].astype(F32))
        u, qft, kf, vft, cc, stats, qst, ks, vst = _in_proj(
            h, norm1[i][None], w_in[i, :, :o0].astype(BF16), w_fox, w_in[i, :, o2:].astype(BF16), fb,
            tm=row_tile, n_heads=fox_heads)
        plan = _fox_plan(stats, tq=fox_tq, tk=attn_tile)
        first = plan[:, :lp // fox_tq, 0:fox_heads:2].transpose(0, 2, 1).reshape(-1)
        y_fox = _fox_attention(first, qft, kf, vft, cc, fox_out_gain[i][None], tq=fox_tq, tk=attn_tile)
        y_sb = _sb_attention(qst, ks, vst, sb_out_gain[i][None], t=attn_tile, group=fox_q_blocks)
        h = _out_proj(h, u, y_fox, y_sb, _block_diag(pool_w[i]).astype(BF16), pool_scale[i][None],
                      w_out[i].astype(BF16), tm=row_tile)
        last = i == depth - 1
        rows = dict(tm=_largest_tile(s_len, FFN_LAST_TILE), first_row=N_META, n_rows=s_len) if last \
            else dict(tm=row_tile, first_row=0, n_rows=lp)
        h = _ffn(h, norm2[i][None], w_gate[i].astype(BF16), w_up[i].astype(BF16),
                 w_down[i].astype(BF16), final_norm[None], final_norm=last, **rows)
    return h


def kernel(x, meta_tokens, norm1, w_in, forget_bias, pool_w, pool_scale, fox_out_gain, sb_out_gain,
           w_out, norm2, w_gate, w_up, w_down, final_norm):
    return _trunk(x, meta_tokens, norm1, w_in, forget_bias, pool_w, pool_scale, fox_out_gain,
                  sb_out_gain, w_out, norm2, w_gate, w_up, w_down, final_norm,
                  attn_tile=ATTN_TILE, fox_q_blocks=FOX_Q_BLOCKS, row_tile=ROW_TILE)
```

```python
import functools
import math

import jax
import jax.numpy as jnp
from jax import lax
from jax.experimental import pallas as pl
from jax.experimental.pallas import tpu as pltpu

HEAD_DIM = 64
N_META = 16
EPS = 1e-6
POOL_WINDOWS = (2, 4, 8, 16)
POOL_GROUP = 64
LANES = 128
BF16_ROWS = 16
HALO = 16
PART_STRIDE = 8
LOG2E = math.log2(math.e)
NEG_BIG = -1e30
ZERO_EXP2 = 152.0
NORM_SLACK = 1.02
VMEM_LIMIT = 56 * 1024 * 1024
F32 = jnp.float32
BF16 = jnp.bfloat16

ATTN_TILE = 256
FOX_Q_BLOCKS = 3
ROW_TILE = 768
FFN_LAST_TILE = 512
FF_CHUNKS = 11
WEIGHT_CAST_COLS = 384
SB_SUM_PARTS = 1


def _dot(a, b):
    return jnp.dot(a, b, preferred_element_type=F32)


def _rms(x, gain):
    ms = jnp.mean(x * x, axis=-1, keepdims=True)
    return x * lax.rsqrt(ms + EPS) * gain


def _split3(x):
    hi = x.astype(BF16)
    r = x - hi.astype(F32)
    mid = r.astype(BF16)
    lo = (r - mid.astype(F32)).astype(BF16)
    return hi, mid, lo


def _pack_parts(x):
    hi, mid, lo = _split3(x)
    return (hi.astype(F32) + pltpu.roll(mid.astype(F32), PART_STRIDE, 1)
            + pltpu.roll(lo.astype(F32), 2 * PART_STRIDE, 1)).astype(BF16)


def _head_sq_norms(q, k):
    width = q.shape[1]
    sq = jnp.concatenate([(x.astype(F32) * x.astype(F32)).astype(BF16) for x in (q, k)], axis=1)
    d = lax.broadcasted_iota(jnp.int32, (2 * width, LANES), 0)
    lane = lax.broadcasted_iota(jnp.int32, (2 * width, LANES), 1)
    target = jnp.where(d < width, PART_STRIDE + d // HEAD_DIM, 2 * PART_STRIDE + (d - width) // HEAD_DIM)
    return _dot(sq, (lane == target).astype(BF16))


def _in_proj_kernel(h_ref, g_ref, w_ref, fb_ref,
                    u_ref, qf_ref, kf_ref, vf_ref, cc_ref, stats_ref, qs_ref, ks_ref, vs_ref,
                    wb_ref, carry_ref, *, tm, width, n_heads):
    i = pl.program_id(1)
    pool_width = u_ref.shape[2]
    fox_cols = 3 * width + LANES

    @pl.when((pl.program_id(0) == 0) & (i == 0))
    def _():
        for c0 in range(0, w_ref.shape[1], WEIGHT_CAST_COLS):
            wb_ref[:, c0:c0 + WEIGHT_CAST_COLS] = w_ref[:, c0:c0 + WEIGHT_CAST_COLS].astype(BF16)

    @pl.when(i == 0)
    def _():
        carry_ref[...] = jnp.zeros_like(carry_ref)

    a = _rms(h_ref[0], g_ref[...]).astype(BF16)
    u_ref[0] = _dot(a, wb_ref[:, :pool_width])

    scale = HEAD_DIM ** -0.5
    pf = _dot(a, wb_ref[:, pool_width:pool_width + fox_cols])
    qf = pf[:, :width] * (scale * LOG2E)
    kf = pf[:, width:2 * width].astype(BF16)
    qf_ref[0] = qf.T.astype(BF16)
    kf_ref[0] = kf
    vf_ref[0] = pf[:, 2 * width:3 * width].T.astype(BF16)
    ps = _dot(a, wb_ref[:, pool_width + fox_cols:])
    qs_ref[0] = (ps[:, :width] * (scale * LOG2E)).T.astype(BF16)
    ks_ref[0] = ps[:, width:2 * width].astype(BF16)
    vs_ref[0] = ps[:, 2 * width:].T.astype(BF16)

    fl = pf[:, 3 * width:] + fb_ref[...]
    lane = lax.broadcasted_iota(jnp.int32, fl.shape, 1)
    log_f = jnp.where(lane < n_heads,
                      (jnp.minimum(fl, 0.0) - jnp.log1p(jnp.exp(-jnp.abs(fl)))) * LOG2E, 0.0)
    row = lax.broadcasted_iota(jnp.int32, (tm, tm), 0)
    col = lax.broadcasted_iota(jnp.int32, (tm, tm), 1)
    sums = _dot((col <= row).astype(BF16), _pack_parts(log_f))
    sums = sums + pltpu.roll(sums, LANES - PART_STRIDE, 1) + pltpu.roll(sums, LANES - 2 * PART_STRIDE, 1)
    c = jnp.where(lane < PART_STRIDE, sums, 0.0) + carry_ref[:1, :]
    carry_ref[...] = jnp.broadcast_to(c[tm - 1:tm, :], carry_ref.shape)
    cc_ref[0] = _pack_parts(c)
    stats_ref[0] = c + _head_sq_norms(qf.astype(BF16), kf)


def _in_proj(h, g, w, fb, *, tm, pool_width, width, n_heads):
    b, lp, d = h.shape
    assert w.shape[1] % WEIGHT_CAST_COLS == 0
    row3 = lambda w: pl.BlockSpec((1, tm, w), lambda bi, i: (bi, i, 0))
    col3 = lambda w: pl.BlockSpec((1, w, tm), lambda bi, i: (bi, 0, i))
    full = lambda arr: pl.BlockSpec(arr.shape, lambda bi, i: (0,) * arr.ndim)
    resident = lambda arr: pl.BlockSpec(arr.shape, lambda bi, i: (0,) * arr.ndim,
                                        pipeline_mode=pl.Buffered(1))
    rows = jax.ShapeDtypeStruct((b, lp, width), BF16)
    cols = jax.ShapeDtypeStruct((b, width, lp), BF16)
    return pl.pallas_call(
        functools.partial(_in_proj_kernel, tm=tm, width=width, n_heads=n_heads),
        grid=(b, lp // tm),
        in_specs=[row3(d), full(g), resident(w), full(fb)],
        out_specs=[row3(pool_width), col3(width), row3(width), col3(width), row3(LANES), row3(LANES),
                   col3(width), row3(width), col3(width)],
        out_shape=[jax.ShapeDtypeStruct((b, lp, pool_width), F32), cols, rows, cols,
                   jax.ShapeDtypeStruct((b, lp, LANES), BF16),
                   jax.ShapeDtypeStruct((b, lp, LANES), F32), cols, rows, cols],
        scratch_shapes=[pltpu.VMEM(w.shape, BF16), pltpu.VMEM((8, LANES), F32)],
        compiler_params=pltpu.CompilerParams(
            dimension_semantics=("arbitrary", "arbitrary"), vmem_limit_bytes=VMEM_LIMIT),
        name="in_proj",
    )(h, g, w, fb)


def _fox_plan_kernel(stats_ref, js_ref, *, tq, tk):
    s = stats_ref[0]
    lp = s.shape[0]
    rows_out = js_ref.shape[1]
    lane = lax.broadcasted_iota(jnp.int32, s.shape, 1)
    k_max_sq = pltpu.roll(jnp.max(s, axis=0, keepdims=True), LANES - 2 * PART_STRIDE, 1)
    q_sq = pltpu.roll(s, LANES - PART_STRIDE, 1)
    spread = jnp.where(lane < PART_STRIDE, jnp.sqrt(q_sq * k_max_sq), 0.0)
    reach = s + (2.0 * NORM_SLACK) * spread + ZERO_EXP2
    tiles = [jnp.max(reach[i * tq:(i + 1) * tq], axis=0, keepdims=True) for i in range(lp // tq)]
    tiles.append(jnp.full((rows_out - len(tiles), LANES), -NEG_BIG, F32))
    reach_max = jnp.concatenate(tiles, axis=0)
    n_key_tiles = lp // tk
    c_end = stats_ref[0, pl.ds(tk - 1, n_key_tiles, stride=tk), :]
    count = jnp.zeros(reach_max.shape, jnp.int32)
    for j in range(n_key_tiles):
        count = count + (c_end[j:j + 1, :] > reach_max).astype(jnp.int32)
    js_ref[0] = jnp.minimum(count, pltpu.roll(count, LANES - 1, 1))


def _fox_plan(stats, *, tq, tk):
    b, lp, _ = stats.shape
    rows_out = -(-(lp // tq) // 8) * 8
    return pl.pallas_call(
        functools.partial(_fox_plan_kernel, tq=tq, tk=tk),
        grid=(b,),
        in_specs=[pl.BlockSpec((1, lp, LANES), lambda bi: (bi, 0, 0))],
        out_specs=pl.BlockSpec((1, rows_out, LANES), lambda bi: (bi, 0, 0)),
        out_shape=jax.ShapeDtypeStruct((b, rows_out, LANES), jnp.int32),
        compiler_params=pltpu.CompilerParams(
            dimension_semantics=("parallel",), vmem_limit_bytes=VMEM_LIMIT),
        name="fox_plan",
    )(stats)


def _head_norm_t(o_t):
    ms = jnp.mean(o_t * o_t, axis=0, keepdims=True)
    return o_t * lax.rsqrt(ms + EPS)


def _fox_kernel(first_ref, qt_ref, k_ref, vt_ref, cc_ref, g_ref, o_ref,
                qa_ref, m_ref, alpha_ref, pt_ref, st_ref, acc_ref, *, tq, tk):
    p = pl.program_id(1)
    qi = pl.program_id(2)
    nb = tq // tk
    n_full = nb * qi
    step = (pl.program_id(0) * pl.num_programs(1) + p) * pl.num_programs(2) + qi
    first = jnp.minimum(first_ref[step], n_full)

    qt = qt_ref[0]
    r = lax.broadcasted_iota(jnp.int32, qt.shape, 0)
    zero = jnp.zeros_like(qt)
    for hh in (0, 1):
        own = (r < HEAD_DIM) if hh == 0 else (r >= HEAD_DIM)
        h = 2 * p + hh
        picks = (r == h) | (r == h + PART_STRIDE) | (r == h + 2 * PART_STRIDE)
        qa_ref[hh, :LANES, :] = jnp.where(own, qt, zero)
        qa_ref[hh, LANES:, :] = jnp.where(picks, -1.0, 0.0).astype(BF16)
    m_ref[...] = jnp.full(m_ref.shape, NEG_BIG, F32)
    acc_ref[...] = jnp.zeros(acc_ref.shape, F32)
    ones = jnp.ones((BF16_ROWS, tk), BF16)

    def pieces_from(c0):
        return [(hh, c) for hh in (0, 1) for c in range(c0, tq, tk)]

    def scores(j, pieces):
        k0 = pl.multiple_of(j * tk, tk)
        kk = jnp.concatenate([k_ref[0, pl.ds(k0, tk), :], cc_ref[0, pl.ds(k0, tk), :]], axis=1)
        return [_dot(kk, qa_ref[hh, :, c:c + tk]) for hh, c in pieces]

    def softmax(st, pieces, masked):
        if masked:
            row = lax.broadcasted_iota(jnp.int32, (tk, tk), 0)
            col = lax.broadcasted_iota(jnp.int32, (tk, tk), 1)
            c0 = pieces[0][1]
            st = [jnp.where(row <= col, s, NEG_BIG) if c == c0 else s for s, (_, c) in zip(st, pieces)]
        weights = []
        for s, (hh, c) in zip(st, pieces):
            m_old = m_ref[hh, :, c:c + tk]
            m_new = jnp.maximum(m_old, jnp.max(s, axis=0, keepdims=True))
            m_ref[hh, :, c:c + tk] = m_new
            weights.append((jnp.exp2(s - m_new).astype(BF16), jnp.exp2(m_old - m_new)))
        return weights

    def accumulate(j, pieces, weights):
        k0 = pl.multiple_of(j * tk, tk)
        for (hh, c), (pt, alpha) in zip(pieces, weights):
            vt = jnp.concatenate(
                [vt_ref[0, hh * HEAD_DIM:(hh + 1) * HEAD_DIM, pl.ds(k0, tk)], ones], axis=0)
            acc_ref[hh, :, c:c + tk] = alpha * acc_ref[hh, :, c:c + tk] + _dot(vt, pt)

    every = pieces_from(0)
    alpha_ref[...] = jnp.ones(alpha_ref.shape, F32)
    pt_ref[...] = jnp.zeros(pt_ref.shape, BF16)

    def put_scores(st):
        for s, (hh, c) in zip(st, every):
            st_ref[hh, :, c:c + tk] = s

    def get_scores():
        return [st_ref[hh, :, c:c + tk] for hh, c in every]

    def put_weights(weights):
        for (pt, alpha), (hh, c) in zip(weights, every):
            pt_ref[hh, :, c:c + tk] = pt
            alpha_ref[hh, :, c:c + tk] = alpha

    def get_weights():
        return [(pt_ref[hh, :, c:c + tk], alpha_ref[hh, :, c:c + tk]) for hh, c in every]

    def full_tile(j, carry):
        ahead = scores(j + 1, every)
        accumulate(jnp.maximum(j - 1, 0), every, get_weights())
        put_weights(softmax(get_scores(), every, False))
        put_scores(ahead)
        return carry

    put_scores(scores(first, every))
    lax.fori_loop(first, n_full, full_tile, 0)
    diagonal = [pieces_from(d * tk) for d in range(nb)]
    st = [get_scores()] + [scores(n_full + d, diagonal[d]) for d in range(1, nb)]
    accumulate(jnp.maximum(n_full - 1, 0), every, get_weights())
    weights = [softmax(st[d], diagonal[d], True) for d in range(nb)]
    for d in range(nb):
        accumulate(n_full + d, diagonal[d], weights[d])

    out = []
    for hh in (0, 1):
        acc = acc_ref[hh]
        out.append(_head_norm_t(acc[:HEAD_DIM] / acc[HEAD_DIM:HEAD_DIM + 1]))
    o_ref[0] = (jnp.concatenate(out, axis=0).T * g_ref[...]).astype(o_ref.dtype)


def _fox_attention(first, qt, k, vt, cc, gain, *, tq, tk):
    b, lp, width = k.shape
    pairs = width // LANES
    grid_spec = pltpu.PrefetchScalarGridSpec(
        num_scalar_prefetch=1,
        grid=(b, pairs, lp // tq),
        in_specs=[pl.BlockSpec((1, LANES, tq), lambda bi, p, qi, first: (bi, p, qi)),
                  pl.BlockSpec((1, lp, LANES), lambda bi, p, qi, first: (bi, 0, p)),
                  pl.BlockSpec((1, LANES, lp), lambda bi, p, qi, first: (bi, p, 0)),
                  pl.BlockSpec((1, lp, LANES), lambda bi, p, qi, first: (bi, 0, 0)),
                  pl.BlockSpec((1, LANES), lambda bi, p, qi, first: (0, p))],
        out_specs=pl.BlockSpec((1, tq, LANES), lambda bi, p, qi, first: (bi, qi, p)),
        scratch_shapes=[pltpu.VMEM((2, 2 * LANES, tq), BF16),
                        pltpu.VMEM((2, 1, tq), F32),
                        pltpu.VMEM((2, 1, tq), F32),
                        pltpu.VMEM((2, tk, tq), BF16),
                        pltpu.VMEM((2, tk, tq), F32),
                        pltpu.VMEM((2, HEAD_DIM + BF16_ROWS, tq), F32)])
    return pl.pallas_call(
        functools.partial(_fox_kernel, tq=tq, tk=tk),
        grid_spec=grid_spec,
        out_shape=jax.ShapeDtypeStruct((b, lp, width), BF16),
        compiler_params=pltpu.CompilerParams(
            dimension_semantics=("parallel", "parallel", "parallel"), vmem_limit_bytes=VMEM_LIMIT),
        name="fox_attention",
    )(first, qt, k, vt, cc, gain)


def _sb_kernel(qt_ref, k_ref, vt_ref, tri_ref, g_ref, o_ref, qa_ref, tail_ref, acc_ref,
               *, t, n_heads, group):
    gi = pl.program_id(1)
    pieces = [(a, h) for a in range(group) for h in range(n_heads)]
    for a, h in pieces:
        qt = qt_ref[0, h * HEAD_DIM:(h + 1) * HEAD_DIM, a * t:(a + 1) * t]
        zero = jnp.zeros_like(qt)
        qa_ref[a, h] = jnp.concatenate([qt, zero] if h % 2 == 0 else [zero, qt], axis=0)

    def key_tiles(back, masked):
        tri = tri_ref[...]
        own = [gi * group + a for a in range(group)]
        exists = [own[a] >= back for a in range(group)]
        k0 = [pl.multiple_of(jnp.maximum(own[a] - back, 0) * t, t) for a in range(group)]
        if masked:
            row = lax.broadcasted_iota(jnp.int32, (t, t), 0)
            col = lax.broadcasted_iota(jnp.int32, (t, t), 1)
            strict = row < col
        z, split, cum, weights, scale = {}, {}, {}, {}, {}

        def logits(a, h):
            z[a, h] = _dot(k_ref[0, pl.ds(k0[a], t), (h // 2) * LANES:(h // 2 + 1) * LANES], qa_ref[a, h])

        def softplus_parts(a, h):
            sp = jnp.maximum(z[a, h], 0.0) + jnp.log2(1.0 + jnp.exp2(-jnp.abs(z[a, h])))
            if masked:
                sp = jnp.where(strict, sp, 0.0)
            parts = [sp.astype(BF16)]
            for _ in range(SB_SUM_PARTS - 1):
                sp = sp - parts[-1].astype(F32)
                parts.append(sp.astype(BF16))
            split[a, h] = jnp.concatenate(parts, axis=0)

        def suffix_sums(a, h):
            cum[a, h] = _dot(tri, split.pop((a, h)))

        def stick_weights(a, h):
            c = cum.pop((a, h))
            e = z.pop((a, h)) - c
            if masked:
                e = jnp.where(strict, e, NEG_BIG)
            weights[a, h] = jnp.exp2(e).astype(BF16)
            if masked:
                tail_ref[a, h] = -c[:1, :]
            else:
                tail = jnp.where(exists[a], tail_ref[a, h], NEG_BIG)
                scale[a, h] = jnp.exp2(tail)
                tail_ref[a, h] = tail - c[:1, :]

        def accumulate(a, h):
            vt = vt_ref[0, h * HEAD_DIM:(h + 1) * HEAD_DIM, pl.ds(k0[a], t)]
            pv = _dot(vt, weights.pop((a, h)))
            acc_ref[a, h] = pv if masked else acc_ref[a, h] + scale.pop((a, h)) * pv

        stages = (logits, softplus_parts, suffix_sums, stick_weights, accumulate)
        for step in range(group + len(stages) - 1):
            for s in reversed(range(len(stages))):
                a = step - s
                if 0 <= a < group:
                    for h in range(n_heads):
                        stages[s](a, h)

    def live():
        return jnp.max(tail_ref[...]) > -ZERO_EXP2

    key_tiles(0, True)

    def cond(carry):
        back, go = carry
        return jnp.logical_and(back <= gi * group + (group - 1), go)

    def body(carry):
        back, _ = carry
        key_tiles(back, False)
        return back + 1, live()

    lax.while_loop(cond, body, (1, live()))
    for a in range(group):
        out = jnp.concatenate([_head_norm_t(acc_ref[a, h]) for h in range(n_heads)], axis=0)
        o_ref[0, a * t:(a + 1) * t, :] = (out.T * g_ref[...]).astype(o_ref.dtype)


def _sb_attention(qt, k, vt, gain, *, t, group):
    b, lp, width = k.shape
    n_heads = width // HEAD_DIM
    idx = jnp.arange(t)
    tri = (idx[None, :] >= idx[:, None]).astype(BF16)
    tri = jnp.concatenate([tri] * SB_SUM_PARTS, axis=1)
    return pl.pallas_call(
        functools.partial(_sb_kernel, t=t, n_heads=n_heads, group=group),
        grid=(b, lp // (group * t)),
        in_specs=[pl.BlockSpec((1, width, group * t), lambda bi, gi: (bi, 0, gi)),
                  pl.BlockSpec((1, lp, width), lambda bi, gi: (bi, 0, 0)),
                  pl.BlockSpec((1, width, lp), lambda bi, gi: (bi, 0, 0)),
                  pl.BlockSpec((t, SB_SUM_PARTS * t), lambda bi, gi: (0, 0)),
                  pl.BlockSpec((1, width), lambda bi, gi: (0, 0))],
        out_specs=pl.BlockSpec((1, group * t, width), lambda bi, gi: (bi, gi, 0)),
        out_shape=jax.ShapeDtypeStruct((b, lp, width), BF16),
        scratch_shapes=[pltpu.VMEM((group, n_heads, LANES, t), BF16),
                        pltpu.VMEM((group, n_heads, 1, t), F32),
                        pltpu.VMEM((group, n_heads, HEAD_DIM, t), F32)],
        compiler_params=pltpu.CompilerParams(
            dimension_semantics=("parallel", "parallel"), vmem_limit_bytes=VMEM_LIMIT),
        name="sb_attention",
    )(qt, k, vt, tri, gain)


def _out_proj_kernel(h_ref, u_ref, up_ref, yf_ref, ys_ref, pw_ref, psc_ref, wo_ref, o_ref, *, tm):
    i = pl.program_id(1)
    u = u_ref[0]
    pool_width = u.shape[1]
    out = h_ref[0] + _dot(jnp.concatenate([yf_ref[0], ys_ref[0]], axis=1), wo_ref[pool_width:, :])
    halo = jnp.where(i == 0, 0.0, up_ref[0])
    x = jnp.concatenate([halo, u], axis=0)
    sums = []
    shift = 1
    for _ in POOL_WINDOWS:
        x = x + pltpu.roll(x, shift, 0)
        sums.append(x[HALO:])
        shift *= 2
    group = lax.broadcasted_iota(jnp.int32, u.shape, 1) // POOL_GROUP
    window_sum = sums[-1]
    window = jnp.full(u.shape, POOL_WINDOWS[-1], jnp.int32)
    for g in range(len(POOL_WINDOWS) - 2, -1, -1):
        window_sum = jnp.where(group == g, sums[g], window_sum)
        window = jnp.where(group == g, POOL_WINDOWS[g], window)
    t1 = i * tm + lax.broadcasted_iota(jnp.int32, u.shape, 0) + 1
    count = jnp.minimum(t1, window).astype(F32)
    d = (window_sum / count - u).astype(BF16)
    y_pool = (_dot(d, pw_ref[...]) * psc_ref[...]).astype(BF16)
    o_ref[0] = out + _dot(y_pool, wo_ref[:pool_width, :])


def _out_proj(h, u, yf, ys, pw, psc, wo, *, tm):
    b, lp, d = h.shape
    row3 = lambda w: pl.BlockSpec((1, tm, w), lambda bi, i: (bi, i, 0))
    full = lambda arr: pl.BlockSpec(arr.shape, lambda bi, i: (0,) * arr.ndim)
    halo_spec = pl.BlockSpec((1, HALO, u.shape[2]),
                             lambda bi, i: (bi, jnp.maximum(i * (tm // HALO) - 1, 0), 0))
    return pl.pallas_call(
        functools.partial(_out_proj_kernel, tm=tm),
        grid=(b, lp // tm),
        in_specs=[row3(d), row3(u.shape[2]), halo_spec, row3(yf.shape[2]), row3(ys.shape[2]),
                  full(pw), full(psc), full(wo)],
        out_specs=row3(d),
        out_shape=jax.ShapeDtypeStruct((b, lp, d), F32),
        compiler_params=pltpu.CompilerParams(
            dimension_semantics=("parallel", "parallel"), vmem_limit_bytes=VMEM_LIMIT),
        name="out_proj",
    )(h, u, u, yf, ys, pw, psc, wo)


def _ffn_kernel(h_ref, g_ref, wg_ref, wu_ref, wd_ref, fg_ref, o_ref, *, chunks, final_norm):
    h = h_ref[0]
    a = _rms(h, g_ref[...]).astype(BF16)
    cw = wg_ref.shape[1] // chunks
    out = h
    for c in range(chunks):
        gate = _dot(a, wg_ref[:, c * cw:(c + 1) * cw])
        up = _dot(a, wu_ref[:, c * cw:(c + 1) * cw])
        act = (gate * jax.nn.sigmoid(gate) * up).astype(BF16)
        out = out + _dot(act, wd_ref[c * cw:(c + 1) * cw, :])
    if final_norm:
        out = _rms(out, fg_ref[...])
    o_ref[0] = out


def _ffn(h, g, wg, wu, wd, fg, *, tm, first_row, n_rows, final_norm):
    b, _, d = h.shape
    rows_in = pl.BlockSpec((pl.Element(1), pl.Element(tm), pl.Element(d)),
                           lambda bi, i: (bi, pl.multiple_of(first_row + i * tm, 8), 0))
    rows_out = pl.BlockSpec((1, tm, d), lambda bi, i: (bi, i, 0))
    resident = lambda arr: pl.BlockSpec(arr.shape, lambda bi, i: (0,) * arr.ndim,
                                        pipeline_mode=pl.Buffered(1))
    return pl.pallas_call(
        functools.partial(_ffn_kernel, chunks=FF_CHUNKS, final_norm=final_norm),
        grid=(b, n_rows // tm),
        in_specs=[rows_in, resident(g), resident(wg), resident(wu), resident(wd), resident(fg)],
        out_specs=rows_out,
        out_shape=jax.ShapeDtypeStruct((b, n_rows, d), F32),
        compiler_params=pltpu.CompilerParams(
            dimension_semantics=("parallel", "parallel"), vmem_limit_bytes=VMEM_LIMIT),
        name="ffn",
    )(h, g, wg, wu, wd, fg)


def _largest_tile(n, limit):
    return max(t for t in range(8, limit + 1, 8) if n % t == 0)


def _block_diag(pool_w):
    groups, cin, cout = pool_w.shape
    out = jnp.zeros((groups * cin, groups * cout), pool_w.dtype)
    for g in range(groups):
        out = out.at[g * cin:(g + 1) * cin, g * cout:(g + 1) * cout].set(pool_w[g])
    return out


def _trunk(x, meta_tokens, norm1, w_in, forget_bias, pool_w, pool_scale, fox_out_gain,
           sb_out_gain, w_out, norm2, w_gate, w_up, w_down, final_norm, *,
           attn_tile, fox_q_blocks, row_tile):
    b, s_len, d = x.shape
    depth = norm1.shape[0]
    l = N_META + s_len
    fox_tq = fox_q_blocks * attn_tile
    step = fox_tq * row_tile // math.gcd(fox_tq, row_tile)
    lp = -(-l // step) * step
    pool_width = pool_scale.shape[1]
    fox_w = fox_out_gain.shape[1]
    fox_heads = fox_w // HEAD_DIM
    assert fox_heads <= PART_STRIDE
    meta = jnp.broadcast_to(meta_tokens[None].astype(x.dtype), (b, N_META, d))
    h = jnp.concatenate([meta, x, jnp.zeros((b, lp - l, d), x.dtype)], axis=1)

    o0 = pool_width
    o1 = o0 + 3 * fox_w
    o2 = o1 + fox_heads
    for i in range(depth):
        w = jnp.concatenate([w_in[i, :, :o2], jnp.zeros((d, LANES - fox_heads), w_in.dtype),
                             w_in[i, :, o2:]], axis=1).astype(F32)
        fb = jnp.zeros((1, LANES), F32).at[0, :fox_heads].set(forget_bias[i].astype(F32))
        u, qft, kf, vft, cc, stats, qst, ks, vst = _in_proj(
            h, norm1[i][None], w, fb, tm=row_tile, pool_width=pool_width, width=fox_w, n_heads=fox_heads)
        plan = _fox_plan(stats, tq=fox_tq, tk=attn_tile)
        first = plan[:, :lp // fox_tq, 0:fox_heads:2].transpose(0, 2, 1).reshape(-1)
        y_fox = _fox_attention(first, qft, kf, vft, cc, fox_out_gain[i][None], tq=fox_tq, tk=attn_tile)
        y_sb = _sb_attention(qst, ks, vst, sb_out_gain[i][None], t=attn_tile, group=fox_q_blocks)
        h = _out_proj(h, u, y_fox, y_sb, _block_diag(pool_w[i]).astype(BF16), pool_scale[i][None],
                      w_out[i].astype(BF16), tm=row_tile)
        last = i == depth - 1
        rows = dict(tm=_largest_tile(s_len, FFN_LAST_TILE), first_row=N_META, n_rows=s_len) if last \
            else dict(tm=row_tile, first_row=0, n_rows=lp)
        h = _ffn(h, norm2[i][None], w_gate[i].astype(BF16), w_up[i].astype(BF16),
                 w_down[i].astype(BF16), final_norm[None], final_norm=last, **rows)
    return h


def kernel(x, meta_tokens, norm1, w_in, forget_bias, pool_w, pool_scale, fox_out_gain, sb_out_gain,
           w_out, norm2, w_gate, w_up, w_down, final_norm):
    return _trunk(x, meta_tokens, norm1, w_in, forget_bias, pool_w, pool_scale, fox_out_gain,
                  sb_out_gain, w_out, norm2, w_gate, w_up, w_down, final_norm,
                  attn_tile=ATTN_TILE, fox_q_blocks=FOX_Q_BLOCKS, row_tile=ROW_TILE)
```

```python
import functools
import math

import jax
import jax.numpy as jnp
from jax import lax
from jax.experimental import pallas as pl
from jax.experimental.pallas import tpu as pltpu

HEAD_DIM = 64
N_META = 16
EPS = 1e-6
POOL_WINDOWS = (2, 4, 8, 16)
POOL_GROUP = 64
LANES = 128
BF16_ROWS = 16
HALO = 16
PART_STRIDE = 8
LOG2E = math.log2(math.e)
NEG_BIG = -1e30
ZERO_EXP2 = 152.0
NORM_SLACK = 1.02
VMEM_LIMIT = 56 * 1024 * 1024
F32 = jnp.float32
BF16 = jnp.bfloat16

ATTN_TILE = 256
FOX_Q_BLOCKS = 3
ROW_TILE = 768
FFN_LAST_TILE = 512
FF_CHUNKS = 11
WEIGHT_CAST_COLS = 384
WEIGHT_CAST_ROWS = 256
SB_SUM_PARTS = 1


def _dot(a, b):
    return jnp.dot(a, b, preferred_element_type=F32)


def _rms(x, gain):
    ms = jnp.mean(x * x, axis=-1, keepdims=True)
    return x * lax.rsqrt(ms + EPS) * gain


def _split3(x):
    hi = x.astype(BF16)
    r = x - hi.astype(F32)
    mid = r.astype(BF16)
    lo = (r - mid.astype(F32)).astype(BF16)
    return hi, mid, lo


def _pack_parts(x):
    hi, mid, lo = _split3(x)
    return (hi.astype(F32) + pltpu.roll(mid.astype(F32), PART_STRIDE, 1)
            + pltpu.roll(lo.astype(F32), 2 * PART_STRIDE, 1)).astype(BF16)


def _head_sq_norms(q, k):
    width = q.shape[1]
    sq = jnp.concatenate([(x.astype(F32) * x.astype(F32)).astype(BF16) for x in (q, k)], axis=1)
    d = lax.broadcasted_iota(jnp.int32, (2 * width, LANES), 0)
    lane = lax.broadcasted_iota(jnp.int32, (2 * width, LANES), 1)
    target = jnp.where(d < width, PART_STRIDE + d // HEAD_DIM, 2 * PART_STRIDE + (d - width) // HEAD_DIM)
    return _dot(sq, (lane == target).astype(BF16))


def _in_proj_kernel(h_ref, g_ref, w_ref, fb_ref,
                    u_ref, qf_ref, kf_ref, vf_ref, cc_ref, stats_ref, qs_ref, ks_ref, vs_ref,
                    wb_ref, carry_ref, *, tm, width, n_heads):
    i = pl.program_id(1)
    pool_width = u_ref.shape[2]
    fox_cols = 3 * width + LANES

    @pl.when((pl.program_id(0) == 0) & (i == 0))
    def _():
        aligned = pool_width + 3 * width

        def copy(dst, src, cols):
            for c0 in range(0, cols, WEIGHT_CAST_COLS):
                n = min(WEIGHT_CAST_COLS, cols - c0)
                wb_ref[:, dst + c0:dst + c0 + n] = w_ref[0, :, src + c0:src + c0 + n].astype(BF16)

        copy(0, 0, aligned)
        forget = w_ref[0, :, aligned:aligned + LANES]
        lane = lax.broadcasted_iota(jnp.int32, forget.shape, 1)
        wb_ref[:, aligned:aligned + LANES] = jnp.where(lane < n_heads, forget, 0.0).astype(BF16)
        copy(aligned + LANES, aligned + n_heads, 3 * width)

    @pl.when(i == 0)
    def _():
        carry_ref[...] = jnp.zeros_like(carry_ref)

    a = _rms(h_ref[0], g_ref[...]).astype(BF16)
    u_ref[0] = _dot(a, wb_ref[:, :pool_width])

    scale = HEAD_DIM ** -0.5
    pf = _dot(a, wb_ref[:, pool_width:pool_width + fox_cols])
    qf = pf[:, :width] * (scale * LOG2E)
    kf = pf[:, width:2 * width].astype(BF16)
    qf_ref[0] = qf.T.astype(BF16)
    kf_ref[0] = kf
    vf_ref[0] = pf[:, 2 * width:3 * width].T.astype(BF16)
    ps = _dot(a, wb_ref[:, pool_width + fox_cols:])
    qs_ref[0] = (ps[:, :width] * (scale * LOG2E)).T.astype(BF16)
    ks_ref[0] = ps[:, width:2 * width].astype(BF16)
    vs_ref[0] = ps[:, 2 * width:].T.astype(BF16)

    fl = pf[:, 3 * width:] + fb_ref[...]
    lane = lax.broadcasted_iota(jnp.int32, fl.shape, 1)
    log_f = jnp.where(lane < n_heads,
                      (jnp.minimum(fl, 0.0) - jnp.log1p(jnp.exp(-jnp.abs(fl)))) * LOG2E, 0.0)
    row = lax.broadcasted_iota(jnp.int32, (tm, tm), 0)
    col = lax.broadcasted_iota(jnp.int32, (tm, tm), 1)
    sums = _dot((col <= row).astype(BF16), _pack_parts(log_f))
    sums = sums + pltpu.roll(sums, LANES - PART_STRIDE, 1) + pltpu.roll(sums, LANES - 2 * PART_STRIDE, 1)
    c = jnp.where(lane < PART_STRIDE, sums, 0.0) + carry_ref[:1, :]
    carry_ref[...] = jnp.broadcast_to(c[tm - 1:tm, :], carry_ref.shape)
    cc_ref[0] = _pack_parts(c)
    stats_ref[0] = c + _head_sq_norms(qf.astype(BF16), kf)


def _in_proj(h, g, w_all, layer, fb, *, tm, pool_width, width, n_heads):
    b, lp, d = h.shape
    bf16_cols = pool_width + 3 * width + LANES + 3 * width
    row3 = lambda w: pl.BlockSpec((1, tm, w), lambda bi, i: (bi, i, 0))
    col3 = lambda w: pl.BlockSpec((1, w, tm), lambda bi, i: (bi, 0, i))
    full = lambda arr: pl.BlockSpec(arr.shape, lambda bi, i: (0,) * arr.ndim)
    layer_weight = pl.BlockSpec((1,) + w_all.shape[1:], lambda bi, i: (layer, 0, 0),
                                pipeline_mode=pl.Buffered(1))
    rows = jax.ShapeDtypeStruct((b, lp, width), BF16)
    cols = jax.ShapeDtypeStruct((b, width, lp), BF16)
    return pl.pallas_call(
        functools.partial(_in_proj_kernel, tm=tm, width=width, n_heads=n_heads),
        grid=(b, lp // tm),
        in_specs=[row3(d), full(g), layer_weight, full(fb)],
        out_specs=[row3(pool_width), col3(width), row3(width), col3(width), row3(LANES), row3(LANES),
                   col3(width), row3(width), col3(width)],
        out_shape=[jax.ShapeDtypeStruct((b, lp, pool_width), F32), cols, rows, cols,
                   jax.ShapeDtypeStruct((b, lp, LANES), BF16),
                   jax.ShapeDtypeStruct((b, lp, LANES), F32), cols, rows, cols],
        scratch_shapes=[pltpu.VMEM((d, bf16_cols), BF16), pltpu.VMEM((8, LANES), F32)],
        compiler_params=pltpu.CompilerParams(
            dimension_semantics=("arbitrary", "arbitrary"), vmem_limit_bytes=VMEM_LIMIT),
        name="in_proj",
    )(h, g, w_all, fb)


def _fox_plan_kernel(stats_ref, js_ref, *, tq, tk):
    s = stats_ref[0]
    lp = s.shape[0]
    rows_out = js_ref.shape[1]
    lane = lax.broadcasted_iota(jnp.int32, s.shape, 1)
    k_max_sq = pltpu.roll(jnp.max(s, axis=0, keepdims=True), LANES - 2 * PART_STRIDE, 1)
    q_sq = pltpu.roll(s, LANES - PART_STRIDE, 1)
    spread = jnp.where(lane < PART_STRIDE, jnp.sqrt(q_sq * k_max_sq), 0.0)
    reach = s + (2.0 * NORM_SLACK) * spread + ZERO_EXP2
    tiles = [jnp.max(reach[i * tq:(i + 1) * tq], axis=0, keepdims=True) for i in range(lp // tq)]
    tiles.append(jnp.full((rows_out - len(tiles), LANES), -NEG_BIG, F32))
    reach_max = jnp.concatenate(tiles, axis=0)
    n_key_tiles = lp // tk
    c_end = stats_ref[0, pl.ds(tk - 1, n_key_tiles, stride=tk), :]
    count = jnp.zeros(reach_max.shape, jnp.int32)
    for j in range(n_key_tiles):
        count = count + (c_end[j:j + 1, :] > reach_max).astype(jnp.int32)
    js_ref[0] = jnp.minimum(count, pltpu.roll(count, LANES - 1, 1))


def _fox_plan(stats, *, tq, tk):
    b, lp, _ = stats.shape
    rows_out = -(-(lp // tq) // 8) * 8
    return pl.pallas_call(
        functools.partial(_fox_plan_kernel, tq=tq, tk=tk),
        grid=(b,),
        in_specs=[pl.BlockSpec((1, lp, LANES), lambda bi: (bi, 0, 0))],
        out_specs=pl.BlockSpec((1, rows_out, LANES), lambda bi: (bi, 0, 0)),
        out_shape=jax.ShapeDtypeStruct((b, rows_out, LANES), jnp.int32),
        compiler_params=pltpu.CompilerParams(
            dimension_semantics=("parallel",), vmem_limit_bytes=VMEM_LIMIT),
        name="fox_plan",
    )(stats)


def _head_norm_t(o_t):
    ms = jnp.mean(o_t * o_t, axis=0, keepdims=True)
    return o_t * lax.rsqrt(ms + EPS)


def _fox_kernel(first_ref, qt_ref, k_ref, vt_ref, cc_ref, g_ref, o_ref,
                qa_ref, m_ref, alpha_ref, pt_ref, st_ref, acc_ref, *, tq, tk):
    p = pl.program_id(1)
    qi = pl.program_id(2)
    nb = tq // tk
    n_full = nb * qi
    step = (pl.program_id(0) * pl.num_programs(1) + p) * pl.num_programs(2) + qi
    first = jnp.minimum(first_ref[step], n_full)

    qt = qt_ref[0]
    r = lax.broadcasted_iota(jnp.int32, qt.shape, 0)
    zero = jnp.zeros_like(qt)
    for hh in (0, 1):
        own = (r < HEAD_DIM) if hh == 0 else (r >= HEAD_DIM)
        h = 2 * p + hh
        picks = (r == h) | (r == h + PART_STRIDE) | (r == h + 2 * PART_STRIDE)
        qa_ref[hh, :LANES, :] = jnp.where(own, qt, zero)
        qa_ref[hh, LANES:, :] = jnp.where(picks, -1.0, 0.0).astype(BF16)
    m_ref[...] = jnp.full(m_ref.shape, NEG_BIG, F32)
    acc_ref[...] = jnp.zeros(acc_ref.shape, F32)
    ones = jnp.ones((BF16_ROWS, tk), BF16)

    def pieces_from(c0):
        return [(hh, c) for hh in (0, 1) for c in range(c0, tq, tk)]

    def scores(j, pieces):
        k0 = pl.multiple_of(j * tk, tk)
        kk = jnp.concatenate([k_ref[0, pl.ds(k0, tk), :], cc_ref[0, pl.ds(k0, tk), :]], axis=1)
        return [_dot(kk, qa_ref[hh, :, c:c + tk]) for hh, c in pieces]

    def softmax(st, pieces, masked):
        if masked:
            row = lax.broadcasted_iota(jnp.int32, (tk, tk), 0)
            col = lax.broadcasted_iota(jnp.int32, (tk, tk), 1)
            c0 = pieces[0][1]
            st = [jnp.where(row <= col, s, NEG_BIG) if c == c0 else s for s, (_, c) in zip(st, pieces)]
        weights = []
        for s, (hh, c) in zip(st, pieces):
            m_old = m_ref[hh, :, c:c + tk]
            m_new = jnp.maximum(m_old, jnp.max(s, axis=0, keepdims=True))
            m_ref[hh, :, c:c + tk] = m_new
            weights.append((jnp.exp2(s - m_new).astype(BF16), jnp.exp2(m_old - m_new)))
        return weights

    def accumulate(j, pieces, weights):
        k0 = pl.multiple_of(j * tk, tk)
        for (hh, c), (pt, alpha) in zip(pieces, weights):
            vt = jnp.concatenate(
                [vt_ref[0, hh * HEAD_DIM:(hh + 1) * HEAD_DIM, pl.ds(k0, tk)], ones], axis=0)
            acc_ref[hh, :, c:c + tk] = alpha * acc_ref[hh, :, c:c + tk] + _dot(vt, pt)

    every = pieces_from(0)
    alpha_ref[...] = jnp.ones(alpha_ref.shape, F32)
    pt_ref[...] = jnp.zeros(pt_ref.shape, BF16)

    def put_scores(st):
        for s, (hh, c) in zip(st, every):
            st_ref[hh, :, c:c + tk] = s

    def get_scores():
        return [st_ref[hh, :, c:c + tk] for hh, c in every]

    def put_weights(weights):
        for (pt, alpha), (hh, c) in zip(weights, every):
            pt_ref[hh, :, c:c + tk] = pt
            alpha_ref[hh, :, c:c + tk] = alpha

    def get_weights():
        return [(pt_ref[hh, :, c:c + tk], alpha_ref[hh, :, c:c + tk]) for hh, c in every]

    def full_tile(j, carry):
        ahead = scores(j + 1, every)
        accumulate(jnp.maximum(j - 1, 0), every, get_weights())
        put_weights(softmax(get_scores(), every, False))
        put_scores(ahead)
        return carry

    put_scores(scores(first, every))
    lax.fori_loop(first, n_full, full_tile, 0)
    diagonal = [pieces_from(d * tk) for d in range(nb)]
    st = [get_scores()] + [scores(n_full + d, diagonal[d]) for d in range(1, nb)]
    accumulate(jnp.maximum(n_full - 1, 0), every, get_weights())
    weights = [softmax(st[d], diagonal[d], True) for d in range(nb)]
    for d in range(nb):
        accumulate(n_full + d, diagonal[d], weights[d])

    out = []
    for hh in (0, 1):
        acc = acc_ref[hh]
        out.append(_head_norm_t(acc[:HEAD_DIM] / acc[HEAD_DIM:HEAD_DIM + 1]))
    o_ref[0] = (jnp.concatenate(out, axis=0).T * g_ref[...]).astype(o_ref.dtype)


def _fox_attention(first, qt, k, vt, cc, gain, *, tq, tk):
    b, lp, width = k.shape
    pairs = width // LANES
    grid_spec = pltpu.PrefetchScalarGridSpec(
        num_scalar_prefetch=1,
        grid=(b, pairs, lp // tq),
        in_specs=[pl.BlockSpec((1, LANES, tq), lambda bi, p, qi, first: (bi, p, qi)),
                  pl.BlockSpec((1, lp, LANES), lambda bi, p, qi, first: (bi, 0, p)),
                  pl.BlockSpec((1, LANES, lp), lambda bi, p, qi, first: (bi, p, 0)),
                  pl.BlockSpec((1, lp, LANES), lambda bi, p, qi, first: (bi, 0, 0)),
                  pl.BlockSpec((1, LANES), lambda bi, p, qi, first: (0, p))],
        out_specs=pl.BlockSpec((1, tq, LANES), lambda bi, p, qi, first: (bi, qi, p)),
        scratch_shapes=[pltpu.VMEM((2, 2 * LANES, tq), BF16),
                        pltpu.VMEM((2, 1, tq), F32),
                        pltpu.VMEM((2, 1, tq), F32),
                        pltpu.VMEM((2, tk, tq), BF16),
                        pltpu.VMEM((2, tk, tq), F32),
                        pltpu.VMEM((2, HEAD_DIM + BF16_ROWS, tq), F32)])
    return pl.pallas_call(
        functools.partial(_fox_kernel, tq=tq, tk=tk),
        grid_spec=grid_spec,
        out_shape=jax.ShapeDtypeStruct((b, lp, width), BF16),
        compiler_params=pltpu.CompilerParams(
            dimension_semantics=("parallel", "parallel", "parallel"), vmem_limit_bytes=VMEM_LIMIT),
        name="fox_attention",
    )(first, qt, k, vt, cc, gain)


def _sb_kernel(qt_ref, k_ref, vt_ref, tri_ref, g_ref, o_ref, qa_ref, tail_ref, acc_ref,
               *, t, n_heads, group):
    gi = pl.program_id(1)
    pieces = [(a, h) for a in range(group) for h in range(n_heads)]
    for a, h in pieces:
        qt = qt_ref[0, h * HEAD_DIM:(h + 1) * HEAD_DIM, a * t:(a + 1) * t]
        zero = jnp.zeros_like(qt)
        qa_ref[a, h] = jnp.concatenate([qt, zero] if h % 2 == 0 else [zero, qt], axis=0)

    def key_tiles(back, masked):
        tri = tri_ref[...]
        own = [gi * group + a for a in range(group)]
        exists = [own[a] >= back for a in range(group)]
        k0 = [pl.multiple_of(jnp.maximum(own[a] - back, 0) * t, t) for a in range(group)]
        if masked:
            row = lax.broadcasted_iota(jnp.int32, (t, t), 0)
            col = lax.broadcasted_iota(jnp.int32, (t, t), 1)
            strict = row < col
        z, split, cum, weights, scale = {}, {}, {}, {}, {}

        def logits(a, h):
            z[a, h] = _dot(k_ref[0, pl.ds(k0[a], t), (h // 2) * LANES:(h // 2 + 1) * LANES], qa_ref[a, h])

        def softplus_parts(a, h):
            sp = jnp.maximum(z[a, h], 0.0) + jnp.log2(1.0 + jnp.exp2(-jnp.abs(z[a, h])))
            if masked:
                sp = jnp.where(strict, sp, 0.0)
            parts = [sp.astype(BF16)]
            for _ in range(SB_SUM_PARTS - 1):
                sp = sp - parts[-1].astype(F32)
                parts.append(sp.astype(BF16))
            split[a, h] = jnp.concatenate(parts, axis=0)

        def suffix_sums(a, h):
            cum[a, h] = _dot(tri, split.pop((a, h)))

        def stick_weights(a, h):
            c = cum.pop((a, h))
            e = z.pop((a, h)) - c
            if masked:
                e = jnp.where(strict, e, NEG_BIG)
            weights[a, h] = jnp.exp2(e).astype(BF16)
            if masked:
                tail_ref[a, h] = -c[:1, :]
            else:
                tail = jnp.where(exists[a], tail_ref[a, h], NEG_BIG)
                scale[a, h] = jnp.exp2(tail)
                tail_ref[a, h] = tail - c[:1, :]

        def accumulate(a, h):
            vt = vt_ref[0, h * HEAD_DIM:(h + 1) * HEAD_DIM, pl.ds(k0[a], t)]
            pv = _dot(vt, weights.pop((a, h)))
            acc_ref[a, h] = pv if masked else acc_ref[a, h] + scale.pop((a, h)) * pv

        stages = (logits, softplus_parts, suffix_sums, stick_weights, accumulate)
        for step in range(group + len(stages) - 1):
            for s in reversed(range(len(stages))):
                a = step - s
                if 0 <= a < group:
                    for h in range(n_heads):
                        stages[s](a, h)

    def live():
        return jnp.max(tail_ref[...]) > -ZERO_EXP2

    key_tiles(0, True)

    def cond(carry):
        back, go = carry
        return jnp.logical_and(back <= gi * group + (group - 1), go)

    def body(carry):
        back, _ = carry
        key_tiles(back, False)
        return back + 1, live()

    lax.while_loop(cond, body, (1, live()))
    for a in range(group):
        out = jnp.concatenate([_head_norm_t(acc_ref[a, h]) for h in range(n_heads)], axis=0)
        o_ref[0, a * t:(a + 1) * t, :] = (out.T * g_ref[...]).astype(o_ref.dtype)


def _sb_attention(qt, k, vt, gain, *, t, group):
    b, lp, width = k.shape
    n_heads = width // HEAD_DIM
    idx = jnp.arange(t)
    tri = (idx[None, :] >= idx[:, None]).astype(BF16)
    tri = jnp.concatenate([tri] * SB_SUM_PARTS, axis=1)
    return pl.pallas_call(
        functools.partial(_sb_kernel, t=t, n_heads=n_heads, group=group),
        grid=(b, lp // (group * t)),
        in_specs=[pl.BlockSpec((1, width, group * t), lambda bi, gi: (bi, 0, gi)),
                  pl.BlockSpec((1, lp, width), lambda bi, gi: (bi, 0, 0)),
                  pl.BlockSpec((1, width, lp), lambda bi, gi: (bi, 0, 0)),
                  pl.BlockSpec((t, SB_SUM_PARTS * t), lambda bi, gi: (0, 0)),
                  pl.BlockSpec((1, width), lambda bi, gi: (0, 0))],
        out_specs=pl.BlockSpec((1, group * t, width), lambda bi, gi: (bi, gi, 0)),
        out_shape=jax.ShapeDtypeStruct((b, lp, width), BF16),
        scratch_shapes=[pltpu.VMEM((group, n_heads, LANES, t), BF16),
                        pltpu.VMEM((group, n_heads, 1, t), F32),
                        pltpu.VMEM((group, n_heads, HEAD_DIM, t), F32)],
        compiler_params=pltpu.CompilerParams(
            dimension_semantics=("parallel", "parallel"), vmem_limit_bytes=VMEM_LIMIT),
        name="sb_attention",
    )(qt, k, vt, tri, gain)


def _out_proj_kernel(h_ref, u_ref, up_ref, yf_ref, ys_ref, pw_ref, psc_ref, w_ref, o_ref, wo_ref, *, tm):
    i = pl.program_id(1)

    @pl.when((pl.program_id(0) == 0) & (i == 0))
    def _():
        for r0 in range(0, wo_ref.shape[0], WEIGHT_CAST_ROWS):
            wo_ref[r0:r0 + WEIGHT_CAST_ROWS, :] = w_ref[0, r0:r0 + WEIGHT_CAST_ROWS, :].astype(BF16)

    u = u_ref[0]
    pool_width = u.shape[1]
    out = h_ref[0] + _dot(jnp.concatenate([yf_ref[0], ys_ref[0]], axis=1), wo_ref[pool_width:, :])
    halo = jnp.where(i == 0, 0.0, up_ref[0])
    x = jnp.concatenate([halo, u], axis=0)
    sums = []
    shift = 1
    for _ in POOL_WINDOWS:
        x = x + pltpu.roll(x, shift, 0)
        sums.append(x[HALO:])
        shift *= 2
    group = lax.broadcasted_iota(jnp.int32, u.shape, 1) // POOL_GROUP
    window_sum = sums[-1]
    window = jnp.full(u.shape, POOL_WINDOWS[-1], jnp.int32)
    for g in range(len(POOL_WINDOWS) - 2, -1, -1):
        window_sum = jnp.where(group == g, sums[g], window_sum)
        window = jnp.where(group == g, POOL_WINDOWS[g], window)
    t1 = i * tm + lax.broadcasted_iota(jnp.int32, u.shape, 0) + 1
    count = jnp.minimum(t1, window).astype(F32)
    d = (window_sum / count - u).astype(BF16)
    y_pool = (_dot(d, pw_ref[...]) * psc_ref[...]).astype(BF16)
    o_ref[0] = out + _dot(y_pool, wo_ref[:pool_width, :])


def _out_proj(h, u, yf, ys, pw, psc, wo_all, layer, *, tm):
    b, lp, d = h.shape
    assert wo_all.shape[1] % WEIGHT_CAST_ROWS == 0
    layer_weight = pl.BlockSpec((1,) + wo_all.shape[1:], lambda bi, i: (layer, 0, 0),
                                pipeline_mode=pl.Buffered(1))
    row3 = lambda w: pl.BlockSpec((1, tm, w), lambda bi, i: (bi, i, 0))
    full = lambda arr: pl.BlockSpec(arr.shape, lambda bi, i: (0,) * arr.ndim)
    halo_spec = pl.BlockSpec((1, HALO, u.shape[2]),
                             lambda bi, i: (bi, jnp.maximum(i * (tm // HALO) - 1, 0), 0))
    return pl.pallas_call(
        functools.partial(_out_proj_kernel, tm=tm),
        grid=(b, lp // tm),
        in_specs=[row3(d), row3(u.shape[2]), halo_spec, row3(yf.shape[2]), row3(ys.shape[2]),
                  full(pw), full(psc), layer_weight],
        out_specs=row3(d),
        out_shape=jax.ShapeDtypeStruct((b, lp, d), F32),
        scratch_shapes=[pltpu.VMEM(wo_all.shape[1:], BF16)],
        compiler_params=pltpu.CompilerParams(
            dimension_semantics=("arbitrary", "arbitrary"), vmem_limit_bytes=VMEM_LIMIT),
        name="out_proj",
    )(h, u, u, yf, ys, pw, psc, wo_all)


def _ffn_kernel(h_ref, g_ref, wg_ref, wu_ref, wd_ref, fg_ref, o_ref, *, chunks, final_norm):
    h = h_ref[0]
    a = _rms(h, g_ref[...]).astype(BF16)
    cw = wg_ref.shape[1] // chunks
    out = h
    for c in range(chunks):
        gate = _dot(a, wg_ref[:, c * cw:(c + 1) * cw])
        up = _dot(a, wu_ref[:, c * cw:(c + 1) * cw])
        act = (gate * jax.nn.sigmoid(gate) * up).astype(BF16)
        out = out + _dot(act, wd_ref[c * cw:(c + 1) * cw, :])
    if final_norm:
        out = _rms(out, fg_ref[...])
    o_ref[0] = out


def _ffn(h, g, wg, wu, wd, fg, *, tm, first_row, n_rows, final_norm):
    b, _, d = h.shape
    rows_in = pl.BlockSpec((pl.Element(1), pl.Element(tm), pl.Element(d)),
                           lambda bi, i: (bi, pl.multiple_of(first_row + i * tm, 8), 0))
    rows_out = pl.BlockSpec((1, tm, d), lambda bi, i: (bi, i, 0))
    resident = lambda arr: pl.BlockSpec(arr.shape, lambda bi, i: (0,) * arr.ndim,
                                        pipeline_mode=pl.Buffered(1))
    return pl.pallas_call(
        functools.partial(_ffn_kernel, chunks=FF_CHUNKS, final_norm=final_norm),
        grid=(b, n_rows // tm),
        in_specs=[rows_in, resident(g), resident(wg), resident(wu), resident(wd), resident(fg)],
        out_specs=rows_out,
        out_shape=jax.ShapeDtypeStruct((b, n_rows, d), F32),
        compiler_params=pltpu.CompilerParams(
            dimension_semantics=("parallel", "parallel"), vmem_limit_bytes=VMEM_LIMIT),
        name="ffn",
    )(h, g, wg, wu, wd, fg)


def _largest_tile(n, limit):
    return max(t for t in range(8, limit + 1, 8) if n % t == 0)


def _block_diag(pool_w):
    groups, cin, cout = pool_w.shape
    out = jnp.zeros((groups * cin, groups * cout), pool_w.dtype)
    for g in range(groups):
        out = out.at[g * cin:(g + 1) * cin, g * cout:(g + 1) * cout].set(pool_w[g])
    return out


def _trunk(x, meta_tokens, norm1, w_in, forget_bias, pool_w, pool_scale, fox_out_gain,
           sb_out_gain, w_out, norm2, w_gate, w_up, w_down, final_norm, *,
           attn_tile, fox_q_blocks, row_tile):
    b, s_len, d = x.shape
    depth = norm1.shape[0]
    l = N_META + s_len
    fox_tq = fox_q_blocks * attn_tile
    step = fox_tq * row_tile // math.gcd(fox_tq, row_tile)
    lp = -(-l // step) * step
    pool_width = pool_scale.shape[1]
    fox_w = fox_out_gain.shape[1]
    fox_heads = fox_w // HEAD_DIM
    assert fox_heads <= PART_STRIDE
    meta = jnp.broadcast_to(meta_tokens[None].astype(x.dtype), (b, N_META, d))
    h = jnp.concatenate([meta, x, jnp.zeros((b, lp - l, d), x.dtype)], axis=1)

    o0 = pool_width
    o1 = o0 + 3 * fox_w
    o2 = o1 + fox_heads
    for i in range(depth):
        fb = jnp.zeros((1, LANES), F32).at[0, :fox_heads].set(forget_bias[i].astype(F32))
        u, qft, kf, vft, cc, stats, qst, ks, vst = _in_proj(
            h, norm1[i][None], w_in.astype(F32), i, fb,
            tm=row_tile, pool_width=pool_width, width=fox_w, n_heads=fox_heads)
        plan = _fox_plan(stats, tq=fox_tq, tk=attn_tile)
        first = plan[:, :lp // fox_tq, 0:fox_heads:2].transpose(0, 2, 1).reshape(-1)
        y_fox = _fox_attention(first, qft, kf, vft, cc, fox_out_gain[i][None], tq=fox_tq, tk=attn_tile)
        y_sb = _sb_attention(qst, ks, vst, sb_out_gain[i][None], t=attn_tile, group=fox_q_blocks)
        h = _out_proj(h, u, y_fox, y_sb, _block_diag(pool_w[i]).astype(BF16), pool_scale[i][None],
                      w_out.astype(F32), i, tm=row_tile)
        last = i == depth - 1
        rows = dict(tm=_largest_tile(s_len, FFN_LAST_TILE), first_row=N_META, n_rows=s_len) if last \
            else dict(tm=row_tile, first_row=0, n_rows=lp)
        h = _ffn(h, norm2[i][None], w_gate[i].astype(BF16), w_up[i].astype(BF16),
                 w_down[i].astype(BF16), final_norm[None], final_norm=last, **rows)
    return h


def kernel(x, meta_tokens, norm1, w_in, forget_bias, pool_w, pool_scale, fox_out_gain, sb_out_gain,
           w_out, norm2, w_gate, w_up, w_down, final_norm):
    return _trunk(x, meta_tokens, norm1, w_in, forget_bias, pool_w, pool_scale, fox_out_gain,
                  sb_out_gain, w_out, norm2, w_gate, w_up, w_down, final_norm,
                  attn_tile=ATTN_TILE, fox_q_blocks=FOX_Q_BLOCKS, row_tile=ROW_TILE)
```

```python
import functools
import math

import jax
import jax.numpy as jnp
from jax import lax
from jax.experimental import pallas as pl
from jax.experimental.pallas import tpu as pltpu

HEAD_DIM = 64
N_META = 16
EPS = 1e-6
POOL_WINDOWS = (2, 4, 8, 16)
POOL_GROUP = 64
LANES = 128
BF16_ROWS = 16
HALO = 16
PART_STRIDE = 8
LOG2E = math.log2(math.e)
NEG_BIG = -1e30
ZERO_EXP2 = 152.0
NORM_SLACK = 1.02
VMEM_LIMIT = 56 * 1024 * 1024
F32 = jnp.float32
BF16 = jnp.bfloat16

ATTN_TILE = 256
FOX_Q_BLOCKS = 3
ROW_TILE = 768
FFN_LAST_TILE = 512
FF_CHUNKS = 11
WEIGHT_CAST_COLS = 384
WEIGHT_CAST_ROWS = 256
SB_SUM_PARTS = 1


def _dot(a, b):
    return jnp.dot(a, b, preferred_element_type=F32)


def _rms(x, gain):
    ms = jnp.mean(x * x, axis=-1, keepdims=True)
    return x * lax.rsqrt(ms + EPS) * gain


def _split3(x):
    hi = x.astype(BF16)
    r = x - hi.astype(F32)
    mid = r.astype(BF16)
    lo = (r - mid.astype(F32)).astype(BF16)
    return hi, mid, lo


def _pack_parts(x):
    hi, mid, lo = _split3(x)
    return (hi.astype(F32) + pltpu.roll(mid.astype(F32), PART_STRIDE, 1)
            + pltpu.roll(lo.astype(F32), 2 * PART_STRIDE, 1)).astype(BF16)


def _head_sq_norms(q, k):
    width = q.shape[1]
    sq = jnp.concatenate([(x.astype(F32) * x.astype(F32)).astype(BF16) for x in (q, k)], axis=1)
    d = lax.broadcasted_iota(jnp.int32, (2 * width, LANES), 0)
    lane = lax.broadcasted_iota(jnp.int32, (2 * width, LANES), 1)
    target = jnp.where(d < width, PART_STRIDE + d // HEAD_DIM, 2 * PART_STRIDE + (d - width) // HEAD_DIM)
    return _dot(sq, (lane == target).astype(BF16))


def _in_proj_kernel(h_ref, g_ref, w_ref, fb_ref, *outs_and_scratch, **static):
    _in_proj_tile(h_ref[0], g_ref, w_ref, fb_ref, *outs_and_scratch, **static)


def _embed_in_proj_kernel(x_ref, x_halo_ref, x_tail_ref, meta_ref, g_ref, w_ref, fb_ref, h_ref,
                          *outs_and_scratch, n_full, **static):
    i = pl.program_id(1)
    rows = h_ref.shape[1] - N_META
    tail = jnp.concatenate([x_tail_ref[0], jnp.zeros((rows - x_tail_ref.shape[1], h_ref.shape[2]), F32)],
                           axis=0)
    body = jnp.where(i < n_full, x_ref[0, :rows], jnp.where(i == n_full, tail, 0.0))
    head = jnp.where(i == 0, meta_ref[...], jnp.where(i <= n_full, x_halo_ref[0], 0.0))
    tile = jnp.concatenate([head, body], axis=0)
    h_ref[0] = tile
    _in_proj_tile(tile, g_ref, w_ref, fb_ref, *outs_and_scratch, **static)


def _in_proj_tile(h, g_ref, w_ref, fb_ref,
                  u_ref, qf_ref, kf_ref, vf_ref, cc_ref, stats_ref, qs_ref, ks_ref, vs_ref,
                  wb_ref, carry_ref, *, tm, width, n_heads):
    i = pl.program_id(1)
    pool_width = u_ref.shape[2]
    fox_cols = 3 * width + LANES

    @pl.when((pl.program_id(0) == 0) & (i == 0))
    def _():
        aligned = pool_width + 3 * width

        def copy(dst, src, cols):
            for c0 in range(0, cols, WEIGHT_CAST_COLS):
                n = min(WEIGHT_CAST_COLS, cols - c0)
                wb_ref[:, dst + c0:dst + c0 + n] = w_ref[0, :, src + c0:src + c0 + n].astype(BF16)

        copy(0, 0, aligned)
        forget = w_ref[0, :, aligned:aligned + LANES]
        lane = lax.broadcasted_iota(jnp.int32, forget.shape, 1)
        wb_ref[:, aligned:aligned + LANES] = jnp.where(lane < n_heads, forget, 0.0).astype(BF16)
        copy(aligned + LANES, aligned + n_heads, 3 * width)

    @pl.when(i == 0)
    def _():
        carry_ref[...] = jnp.zeros_like(carry_ref)

    a = _rms(h, g_ref[...]).astype(BF16)
    u_ref[0] = _dot(a, wb_ref[:, :pool_width])

    scale = HEAD_DIM ** -0.5
    pf = _dot(a, wb_ref[:, pool_width:pool_width + fox_cols])
    qf = pf[:, :width] * (scale * LOG2E)
    kf = pf[:, width:2 * width].astype(BF16)
    qf_ref[0] = qf.T.astype(BF16)
    kf_ref[0] = kf
    vf_ref[0] = pf[:, 2 * width:3 * width].T.astype(BF16)
    ps = _dot(a, wb_ref[:, pool_width + fox_cols:])
    qs_ref[0] = (ps[:, :width] * (scale * LOG2E)).T.astype(BF16)
    ks_ref[0] = ps[:, width:2 * width].astype(BF16)
    vs_ref[0] = ps[:, 2 * width:].T.astype(BF16)

    fl = pf[:, 3 * width:] + fb_ref[...]
    lane = lax.broadcasted_iota(jnp.int32, fl.shape, 1)
    log_f = jnp.where(lane < n_heads,
                      (jnp.minimum(fl, 0.0) - jnp.log1p(jnp.exp(-jnp.abs(fl)))) * LOG2E, 0.0)
    row = lax.broadcasted_iota(jnp.int32, (tm, tm), 0)
    col = lax.broadcasted_iota(jnp.int32, (tm, tm), 1)
    sums = _dot((col <= row).astype(BF16), _pack_parts(log_f))
    sums = sums + pltpu.roll(sums, LANES - PART_STRIDE, 1) + pltpu.roll(sums, LANES - 2 * PART_STRIDE, 1)
    c = jnp.where(lane < PART_STRIDE, sums, 0.0) + carry_ref[:1, :]
    carry_ref[...] = jnp.broadcast_to(c[tm - 1:tm, :], carry_ref.shape)
    cc_ref[0] = _pack_parts(c)
    stats_ref[0] = c + _head_sq_norms(qf.astype(BF16), kf)


def _in_proj(h, g, w_all, layer, fb, *, tm, pool_width, width, n_heads, embed=None):
    if embed is None:
        b, lp, d = h.shape
    else:
        x, meta, lp = embed
        b, s_len, d = x.shape
    bf16_cols = pool_width + 3 * width + LANES + 3 * width
    row3 = lambda w: pl.BlockSpec((1, tm, w), lambda bi, i: (bi, i, 0))
    col3 = lambda w: pl.BlockSpec((1, w, tm), lambda bi, i: (bi, 0, i))
    full = lambda arr: pl.BlockSpec(arr.shape, lambda bi, i: (0,) * arr.ndim)
    layer_weight = pl.BlockSpec((1,) + w_all.shape[1:], lambda bi, i: (layer, 0, 0),
                                pipeline_mode=pl.Buffered(1))
    rows = jax.ShapeDtypeStruct((b, lp, width), BF16)
    cols = jax.ShapeDtypeStruct((b, width, lp), BF16)
    static = dict(tm=tm, width=width, n_heads=n_heads)
    in_specs = [full(g), layer_weight, full(fb)]
    out_specs = [row3(pool_width), col3(width), row3(width), col3(width), row3(LANES), row3(LANES),
                 col3(width), row3(width), col3(width)]
    out_shape = [jax.ShapeDtypeStruct((b, lp, pool_width), F32), cols, rows, cols,
                 jax.ShapeDtypeStruct((b, lp, LANES), BF16),
                 jax.ShapeDtypeStruct((b, lp, LANES), F32), cols, rows, cols]
    if embed is None:
        body, operands = functools.partial(_in_proj_kernel, **static), (h,)
        in_specs = [row3(d)] + in_specs
    else:
        n_full = s_len // tm
        rem = s_len - n_full * tm
        assert N_META == HALO and rem % 8 == 0 and 0 < rem <= tm - N_META and s_len % N_META == 0
        last_halo = s_len // N_META - 1
        x_specs = [
            pl.BlockSpec((1, tm, d), lambda bi, i: (bi, jnp.minimum(i, n_full - 1), 0)),
            pl.BlockSpec((1, N_META, d),
                         lambda bi, i: (bi, jnp.clip(i * (tm // N_META) - 1, 0, last_halo), 0)),
            pl.BlockSpec((pl.Element(1), pl.Element(rem), pl.Element(d)),
                         lambda bi, i: (bi, n_full * tm, 0)),
        ]
        body = functools.partial(_embed_in_proj_kernel, n_full=n_full, **static)
        operands = (x, x, x, meta)
        in_specs = x_specs + [full(meta)] + in_specs
        out_specs = [row3(d)] + out_specs
        out_shape = [jax.ShapeDtypeStruct((b, lp, d), F32)] + out_shape
    return pl.pallas_call(
        body,
        grid=(b, lp // tm),
        in_specs=in_specs,
        out_specs=out_specs,
        out_shape=out_shape,
        scratch_shapes=[pltpu.VMEM((d, bf16_cols), BF16), pltpu.VMEM((8, LANES), F32)],
        compiler_params=pltpu.CompilerParams(
            dimension_semantics=("arbitrary", "arbitrary"), vmem_limit_bytes=VMEM_LIMIT),
        name="in_proj",
    )(*operands, g, w_all, fb)


def _fox_plan_kernel(stats_ref, js_ref, *, tq, tk):
    s = stats_ref[0]
    lp = s.shape[0]
    rows_out = js_ref.shape[1]
    lane = lax.broadcasted_iota(jnp.int32, s.shape, 1)
    k_max_sq = pltpu.roll(jnp.max(s, axis=0, keepdims=True), LANES - 2 * PART_STRIDE, 1)
    q_sq = pltpu.roll(s, LANES - PART_STRIDE, 1)
    spread = jnp.where(lane < PART_STRIDE, jnp.sqrt(q_sq * k_max_sq), 0.0)
    reach = s + (2.0 * NORM_SLACK) * spread + ZERO_EXP2
    tiles = [jnp.max(reach[i * tq:(i + 1) * tq], axis=0, keepdims=True) for i in range(lp // tq)]
    tiles.append(jnp.full((rows_out - len(tiles), LANES), -NEG_BIG, F32))
    reach_max = jnp.concatenate(tiles, axis=0)
    n_key_tiles = lp // tk
    c_end = stats_ref[0, pl.ds(tk - 1, n_key_tiles, stride=tk), :]
    count = jnp.zeros(reach_max.shape, jnp.int32)
    for j in range(n_key_tiles):
        count = count + (c_end[j:j + 1, :] > reach_max).astype(jnp.int32)
    js_ref[0] = jnp.minimum(count, pltpu.roll(count, LANES - 1, 1))


def _fox_plan(stats, *, tq, tk):
    b, lp, _ = stats.shape
    rows_out = -(-(lp // tq) // 8) * 8
    return pl.pallas_call(
        functools.partial(_fox_plan_kernel, tq=tq, tk=tk),
        grid=(b,),
        in_specs=[pl.BlockSpec((1, lp, LANES), lambda bi: (bi, 0, 0))],
        out_specs=pl.BlockSpec((1, rows_out, LANES), lambda bi: (bi, 0, 0)),
        out_shape=jax.ShapeDtypeStruct((b, rows_out, LANES), jnp.int32),
        compiler_params=pltpu.CompilerParams(
            dimension_semantics=("parallel",), vmem_limit_bytes=VMEM_LIMIT),
        name="fox_plan",
    )(stats)


def _head_norm_t(o_t):
    ms = jnp.mean(o_t * o_t, axis=0, keepdims=True)
    return o_t * lax.rsqrt(ms + EPS)


def _fox_kernel(first_ref, qt_ref, k_ref, vt_ref, cc_ref, g_ref, o_ref,
                qa_ref, m_ref, alpha_ref, pt_ref, st_ref, acc_ref, *, tq, tk):
    p = pl.program_id(1)
    qi = pl.program_id(2)
    nb = tq // tk
    n_full = nb * qi
    step = (pl.program_id(0) * pl.num_programs(1) + p) * pl.num_programs(2) + qi
    first = jnp.minimum(first_ref[step], n_full)

    qt = qt_ref[0]
    r = lax.broadcasted_iota(jnp.int32, qt.shape, 0)
    zero = jnp.zeros_like(qt)
    for hh in (0, 1):
        own = (r < HEAD_DIM) if hh == 0 else (r >= HEAD_DIM)
        h = 2 * p + hh
        picks = (r == h) | (r == h + PART_STRIDE) | (r == h + 2 * PART_STRIDE)
        qa_ref[hh, :LANES, :] = jnp.where(own, qt, zero)
        qa_ref[hh, LANES:, :] = jnp.where(picks, -1.0, 0.0).astype(BF16)
    m_ref[...] = jnp.full(m_ref.shape, NEG_BIG, F32)
    acc_ref[...] = jnp.zeros(acc_ref.shape, F32)
    ones = jnp.ones((BF16_ROWS, tk), BF16)

    def pieces_from(c0):
        return [(hh, c) for hh in (0, 1) for c in range(c0, tq, tk)]

    def scores(j, pieces):
        k0 = pl.multiple_of(j * tk, tk)
        kk = jnp.concatenate([k_ref[0, pl.ds(k0, tk), :], cc_ref[0, pl.ds(k0, tk), :]], axis=1)
        return [_dot(kk, qa_ref[hh, :, c:c + tk]) for hh, c in pieces]

    def softmax(st, pieces, masked):
        if masked:
            row = lax.broadcasted_iota(jnp.int32, (tk, tk), 0)
            col = lax.broadcasted_iota(jnp.int32, (tk, tk), 1)
            c0 = pieces[0][1]
            st = [jnp.where(row <= col, s, NEG_BIG) if c == c0 else s for s, (_, c) in zip(st, pieces)]
        weights = []
        for s, (hh, c) in zip(st, pieces):
            m_old = m_ref[hh, :, c:c + tk]
            m_new = jnp.maximum(m_old, jnp.max(s, axis=0, keepdims=True))
            m_ref[hh, :, c:c + tk] = m_new
            weights.append((jnp.exp2(s - m_new).astype(BF16), jnp.exp2(m_old - m_new)))
        return weights

    def accumulate(j, pieces, weights):
        k0 = pl.multiple_of(j * tk, tk)
        for (hh, c), (pt, alpha) in zip(pieces, weights):
            vt = jnp.concatenate(
                [vt_ref[0, hh * HEAD_DIM:(hh + 1) * HEAD_DIM, pl.ds(k0, tk)], ones], axis=0)
            acc_ref[hh, :, c:c + tk] = alpha * acc_ref[hh, :, c:c + tk] + _dot(vt, pt)

    every = pieces_from(0)
    alpha_ref[...] = jnp.ones(alpha_ref.shape, F32)
    pt_ref[...] = jnp.zeros(pt_ref.shape, BF16)

    def put_scores(st):
        for s, (hh, c) in zip(st, every):
            st_ref[hh, :, c:c + tk] = s

    def get_scores():
        return [st_ref[hh, :, c:c + tk] for hh, c in every]

    def put_weights(weights):
        for (pt, alpha), (hh, c) in zip(weights, every):
            pt_ref[hh, :, c:c + tk] = pt
            alpha_ref[hh, :, c:c + tk] = alpha

    def get_weights():
        return [(pt_ref[hh, :, c:c + tk], alpha_ref[hh, :, c:c + tk]) for hh, c in every]

    def full_tile(j, carry):
        ahead = scores(j + 1, every)
        accumulate(jnp.maximum(j - 1, 0), every, get_weights())
        put_weights(softmax(get_scores(), every, False))
        put_scores(ahead)
        return carry

    put_scores(scores(first, every))
    lax.fori_loop(first, n_full, full_tile, 0)
    diagonal = [pieces_from(d * tk) for d in range(nb)]
    st = [get_scores()] + [scores(n_full + d, diagonal[d]) for d in range(1, nb)]
    accumulate(jnp.maximum(n_full - 1, 0), every, get_weights())
    weights = [softmax(st[d], diagonal[d], True) for d in range(nb)]
    for d in range(nb):
        accumulate(n_full + d, diagonal[d], weights[d])

    out = []
    for hh in (0, 1):
        acc = acc_ref[hh]
        out.append(_head_norm_t(acc[:HEAD_DIM] / acc[HEAD_DIM:HEAD_DIM + 1]))
    o_ref[0] = (jnp.concatenate(out, axis=0).T * g_ref[...]).astype(o_ref.dtype)


def _fox_attention(first, qt, k, vt, cc, gain, *, tq, tk):
    b, lp, width = k.shape
    pairs = width // LANES
    grid_spec = pltpu.PrefetchScalarGridSpec(
        num_scalar_prefetch=1,
        grid=(b, pairs, lp // tq),
        in_specs=[pl.BlockSpec((1, LANES, tq), lambda bi, p, qi, first: (bi, p, qi)),
                  pl.BlockSpec((1, lp, LANES), lambda bi, p, qi, first: (bi, 0, p)),
                  pl.BlockSpec((1, LANES, lp), lambda bi, p, qi, first: (bi, p, 0)),
                  pl.BlockSpec((1, lp, LANES), lambda bi, p, qi, first: (bi, 0, 0)),
                  pl.BlockSpec((1, LANES), lambda bi, p, qi, first: (0, p))],
        out_specs=pl.BlockSpec((1, tq, LANES), lambda bi, p, qi, first: (bi, qi, p)),
        scratch_shapes=[pltpu.VMEM((2, 2 * LANES, tq), BF16),
                        pltpu.VMEM((2, 1, tq), F32),
                        pltpu.VMEM((2, 1, tq), F32),
                        pltpu.VMEM((2, tk, tq), BF16),
                        pltpu.VMEM((2, tk, tq), F32),
                        pltpu.VMEM((2, HEAD_DIM + BF16_ROWS, tq), F32)])
    return pl.pallas_call(
        functools.partial(_fox_kernel, tq=tq, tk=tk),
        grid_spec=grid_spec,
        out_shape=jax.ShapeDtypeStruct((b, lp, width), BF16),
        compiler_params=pltpu.CompilerParams(
            dimension_semantics=("parallel", "parallel", "parallel"), vmem_limit_bytes=VMEM_LIMIT),
        name="fox_attention",
    )(first, qt, k, vt, cc, gain)


def _sb_kernel(qt_ref, k_ref, vt_ref, tri_ref, g_ref, o_ref, qa_ref, tail_ref, acc_ref,
               *, t, n_heads, group):
    gi = pl.program_id(1)
    pieces = [(a, h) for a in range(group) for h in range(n_heads)]
    for a, h in pieces:
        qt = qt_ref[0, h * HEAD_DIM:(h + 1) * HEAD_DIM, a * t:(a + 1) * t]
        zero = jnp.zeros_like(qt)
        qa_ref[a, h] = jnp.concatenate([qt, zero] if h % 2 == 0 else [zero, qt], axis=0)

    def key_tiles(back, masked):
        tri = tri_ref[...]
        own = [gi * group + a for a in range(group)]
        exists = [own[a] >= back for a in range(group)]
        k0 = [pl.multiple_of(jnp.maximum(own[a] - back, 0) * t, t) for a in range(group)]
        if masked:
            row = lax.broadcasted_iota(jnp.int32, (t, t), 0)
            col = lax.broadcasted_iota(jnp.int32, (t, t), 1)
            strict = row < col
        z, split, cum, weights, scale = {}, {}, {}, {}, {}

        def logits(a, h):
            z[a, h] = _dot(k_ref[0, pl.ds(k0[a], t), (h // 2) * LANES:(h // 2 + 1) * LANES], qa_ref[a, h])

        def softplus_parts(a, h):
            sp = jnp.maximum(z[a, h], 0.0) + jnp.log2(1.0 + jnp.exp2(-jnp.abs(z[a, h])))
            if masked:
                sp = jnp.where(strict, sp, 0.0)
            parts = [sp.astype(BF16)]
            for _ in range(SB_SUM_PARTS - 1):
                sp = sp - parts[-1].astype(F32)
                parts.append(sp.astype(BF16))
            split[a, h] = jnp.concatenate(parts, axis=0)

        def suffix_sums(a, h):
            cum[a, h] = _dot(tri, split.pop((a, h)))

        def stick_weights(a, h):
            c = cum.pop((a, h))
            e = z.pop((a, h)) - c
            if masked:
                e = jnp.where(strict, e, NEG_BIG)
            weights[a, h] = jnp.exp2(e).astype(BF16)
            if masked:
                tail_ref[a, h] = -c[:1, :]
            else:
                tail = jnp.where(exists[a], tail_ref[a, h], NEG_BIG)
                scale[a, h] = jnp.exp2(tail)
                tail_ref[a, h] = tail - c[:1, :]

        def accumulate(a, h):
            vt = vt_ref[0, h * HEAD_DIM:(h + 1) * HEAD_DIM, pl.ds(k0[a], t)]
            pv = _dot(vt, weights.pop((a, h)))
            acc_ref[a, h] = pv if masked else acc_ref[a, h] + scale.pop((a, h)) * pv

        stages = (logits, softplus_parts, suffix_sums, stick_weights, accumulate)
        for step in range(group + len(stages) - 1):
            for s in reversed(range(len(stages))):
                a = step - s
                if 0 <= a < group:
                    for h in range(n_heads):
                        stages[s](a, h)

    def live():
        return jnp.max(tail_ref[...]) > -ZERO_EXP2

    key_tiles(0, True)

    def cond(carry):
        back, go = carry
        return jnp.logical_and(back <= gi * group + (group - 1), go)

    def body(carry):
        back, _ = carry
        key_tiles(back, False)
        return back + 1, live()

    lax.while_loop(cond, body, (1, live()))
    for a in range(group):
        out = jnp.concatenate([_head_norm_t(acc_ref[a, h]) for h in range(n_heads)], axis=0)
        o_ref[0, a * t:(a + 1) * t, :] = (out.T * g_ref[...]).astype(o_ref.dtype)


def _sb_attention(qt, k, vt, gain, *, t, group):
    b, lp, width = k.shape
    n_heads = width // HEAD_DIM
    idx = jnp.arange(t)
    tri = (idx[None, :] >= idx[:, None]).astype(BF16)
    tri = jnp.concatenate([tri] * SB_SUM_PARTS, axis=1)
    return pl.pallas_call(
        functools.partial(_sb_kernel, t=t, n_heads=n_heads, group=group),
        grid=(b, lp // (group * t)),
        in_specs=[pl.BlockSpec((1, width, group * t), lambda bi, gi: (bi, 0, gi)),
                  pl.BlockSpec((1, lp, width), lambda bi, gi: (bi, 0, 0)),
                  pl.BlockSpec((1, width, lp), lambda bi, gi: (bi, 0, 0)),
                  pl.BlockSpec((t, SB_SUM_PARTS * t), lambda bi, gi: (0, 0)),
                  pl.BlockSpec((1, width), lambda bi, gi: (0, 0))],
        out_specs=pl.BlockSpec((1, group * t, width), lambda bi, gi: (bi, gi, 0)),
        out_shape=jax.ShapeDtypeStruct((b, lp, width), BF16),
        scratch_shapes=[pltpu.VMEM((group, n_heads, LANES, t), BF16),
                        pltpu.VMEM((group, n_heads, 1, t), F32),
                        pltpu.VMEM((group, n_heads, HEAD_DIM, t), F32)],
        compiler_params=pltpu.CompilerParams(
            dimension_semantics=("parallel", "parallel"), vmem_limit_bytes=VMEM_LIMIT),
        name="sb_attention",
    )(qt, k, vt, tri, gain)


def _out_proj_kernel(h_ref, u_ref, up_ref, yf_ref, ys_ref, pw_ref, psc_ref, w_ref, o_ref, wo_ref, *, tm):
    i = pl.program_id(1)

    @pl.when((pl.program_id(0) == 0) & (i == 0))
    def _():
        for r0 in range(0, wo_ref.shape[0], WEIGHT_CAST_ROWS):
            wo_ref[r0:r0 + WEIGHT_CAST_ROWS, :] = w_ref[0, r0:r0 + WEIGHT_CAST_ROWS, :].astype(BF16)

    u = u_ref[0]
    pool_width = u.shape[1]
    out = h_ref[0] + _dot(jnp.concatenate([yf_ref[0], ys_ref[0]], axis=1), wo_ref[pool_width:, :])
    halo = jnp.where(i == 0, 0.0, up_ref[0])
    x = jnp.concatenate([halo, u], axis=0)
    sums = []
    shift = 1
    for _ in POOL_WINDOWS:
        x = x + pltpu.roll(x, shift, 0)
        sums.append(x[HALO:])
        shift *= 2
    group = lax.broadcasted_iota(jnp.int32, u.shape, 1) // POOL_GROUP
    window_sum = sums[-1]
    window = jnp.full(u.shape, POOL_WINDOWS[-1], jnp.int32)
    for g in range(len(POOL_WINDOWS) - 2, -1, -1):
        window_sum = jnp.where(group == g, sums[g], window_sum)
        window = jnp.where(group == g, POOL_WINDOWS[g], window)
    t1 = i * tm + lax.broadcasted_iota(jnp.int32, u.shape, 0) + 1
    count = jnp.minimum(t1, window).astype(F32)
    d = (window_sum / count - u).astype(BF16)
    y_pool = (_dot(d, pw_ref[...]) * psc_ref[...]).astype(BF16)
    o_ref[0] = out + _dot(y_pool, wo_ref[:pool_width, :])


def _out_proj(h, u, yf, ys, pw, psc, wo_all, layer, *, tm):
    b, lp, d = h.shape
    assert wo_all.shape[1] % WEIGHT_CAST_ROWS == 0
    layer_weight = pl.BlockSpec((1,) + wo_all.shape[1:], lambda bi, i: (layer, 0, 0),
                                pipeline_mode=pl.Buffered(1))
    row3 = lambda w: pl.BlockSpec((1, tm, w), lambda bi, i: (bi, i, 0))
    full = lambda arr: pl.BlockSpec(arr.shape, lambda bi, i: (0,) * arr.ndim)
    halo_spec = pl.BlockSpec((1, HALO, u.shape[2]),
                             lambda bi, i: (bi, jnp.maximum(i * (tm // HALO) - 1, 0), 0))
    return pl.pallas_call(
        functools.partial(_out_proj_kernel, tm=tm),
        grid=(b, lp // tm),
        in_specs=[row3(d), row3(u.shape[2]), halo_spec, row3(yf.shape[2]), row3(ys.shape[2]),
                  full(pw), full(psc), layer_weight],
        out_specs=row3(d),
        out_shape=jax.ShapeDtypeStruct((b, lp, d), F32),
        scratch_shapes=[pltpu.VMEM(wo_all.shape[1:], BF16)],
        compiler_params=pltpu.CompilerParams(
            dimension_semantics=("arbitrary", "arbitrary"), vmem_limit_bytes=VMEM_LIMIT),
        name="out_proj",
    )(h, u, u, yf, ys, pw, psc, wo_all)


def _ffn_kernel(h_ref, g_ref, wg_ref, wu_ref, wd_ref, fg_ref, o_ref, *, chunks, final_norm):
    h = h_ref[0]
    a = _rms(h, g_ref[...]).astype(BF16)
    cw = wg_ref.shape[2] // chunks
    out = h
    for c in range(chunks):
        gate = _dot(a, wg_ref[0, :, c * cw:(c + 1) * cw])
        up = _dot(a, wu_ref[0, :, c * cw:(c + 1) * cw])
        act = (gate * jax.nn.sigmoid(gate) * up).astype(BF16)
        out = out + _dot(act, wd_ref[0, c * cw:(c + 1) * cw, :])
    if final_norm:
        out = _rms(out, fg_ref[...])
    o_ref[0] = out


def _ffn(h, g, wg, wu, wd, layer, fg, *, tm, first_row, n_rows, final_norm):
    b, _, d = h.shape
    rows_in = pl.BlockSpec((pl.Element(1), pl.Element(tm), pl.Element(d)),
                           lambda bi, i: (bi, pl.multiple_of(first_row + i * tm, 8), 0))
    rows_out = pl.BlockSpec((1, tm, d), lambda bi, i: (bi, i, 0))
    resident = lambda arr: pl.BlockSpec(arr.shape, lambda bi, i: (0,) * arr.ndim,
                                        pipeline_mode=pl.Buffered(1))
    of_layer = lambda arr: pl.BlockSpec((1,) + arr.shape[1:], lambda bi, i: (layer, 0, 0),
                                        pipeline_mode=pl.Buffered(1))
    return pl.pallas_call(
        functools.partial(_ffn_kernel, chunks=FF_CHUNKS, final_norm=final_norm),
        grid=(b, n_rows // tm),
        in_specs=[rows_in, resident(g), of_layer(wg), of_layer(wu), of_layer(wd), resident(fg)],
        out_specs=rows_out,
        out_shape=jax.ShapeDtypeStruct((b, n_rows, d), F32),
        compiler_params=pltpu.CompilerParams(
            dimension_semantics=("parallel", "parallel"), vmem_limit_bytes=VMEM_LIMIT),
        name="ffn",
    )(h, g, wg, wu, wd, fg)


def _largest_tile(n, limit):
    return max(t for t in range(8, limit + 1, 8) if n % t == 0)


def _block_diag(pool_w):
    groups, cin, cout = pool_w.shape
    out = jnp.zeros((groups * cin, groups * cout), pool_w.dtype)
    for g in range(groups):
        out = out.at[g * cin:(g + 1) * cin, g * cout:(g + 1) * cout].set(pool_w[g])
    return out


def _trunk(x, meta_tokens, norm1, w_in, forget_bias, pool_w, pool_scale, fox_out_gain,
           sb_out_gain, w_out, norm2, w_gate, w_up, w_down, final_norm, *,
           attn_tile, fox_q_blocks, row_tile):
    b, s_len, d = x.shape
    depth = norm1.shape[0]
    l = N_META + s_len
    fox_tq = fox_q_blocks * attn_tile
    step = fox_tq * row_tile // math.gcd(fox_tq, row_tile)
    lp = -(-l // step) * step
    pool_width = pool_scale.shape[1]
    fox_w = fox_out_gain.shape[1]
    fox_heads = fox_w // HEAD_DIM
    assert fox_heads <= PART_STRIDE

    o0 = pool_width
    o1 = o0 + 3 * fox_w
    o2 = o1 + fox_heads
    for i in range(depth):
        fb = jnp.zeros((1, LANES), F32).at[0, :fox_heads].set(forget_bias[i].astype(F32))
        proj = dict(tm=row_tile, pool_width=pool_width, width=fox_w, n_heads=fox_heads)
        if i == 0:
            h, *outs = _in_proj(None, norm1[i][None], w_in.astype(F32), i, fb,
                                embed=(x.astype(F32), meta_tokens.astype(F32), lp), **proj)
        else:
            outs = _in_proj(h, norm1[i][None], w_in.astype(F32), i, fb, **proj)
        u, qft, kf, vft, cc, stats, qst, ks, vst = outs
        plan = _fox_plan(stats, tq=fox_tq, tk=attn_tile)
        first = plan[:, :lp // fox_tq, 0:fox_heads:2].transpose(0, 2, 1).reshape(-1)
        y_fox = _fox_attention(first, qft, kf, vft, cc, fox_out_gain[i][None], tq=fox_tq, tk=attn_tile)
        y_sb = _sb_attention(qst, ks, vst, sb_out_gain[i][None], t=attn_tile, group=fox_q_blocks)
        h = _out_proj(h, u, y_fox, y_sb, _block_diag(pool_w[i]).astype(BF16), pool_scale[i][None],
                      w_out.astype(F32), i, tm=row_tile)
        last = i == depth - 1
        rows = dict(tm=_largest_tile(s_len, FFN_LAST_TILE), first_row=N_META, n_rows=s_len) if last \
            else dict(tm=row_tile, first_row=0, n_rows=lp)
        h = _ffn(h, norm2[i][None], w_gate.astype(BF16), w_up.astype(BF16), w_down.astype(BF16), i,
                 final_norm[None], final_norm=last, **rows)
    return h


def kernel(x, meta_tokens, norm1, w_in, forget_bias, pool_w, pool_scale, fox_out_gain, sb_out_gain,
           w_out, norm2, w_gate, w_up, w_down, final_norm):
    return _trunk(x, meta_tokens, norm1, w_in, forget_bias, pool_w, pool_scale, fox_out_gain,
                  sb_out_gain, w_out, norm2, w_gate, w_up, w_down, final_norm,
                  attn_tile=ATTN_TILE, fox_q_blocks=FOX_Q_BLOCKS, row_tile=ROW_TILE)
```

```python
import functools
import math

import jax
import jax.numpy as jnp
from jax import lax
from jax.experimental import pallas as pl
from jax.experimental.pallas import tpu as pltpu

HEAD_DIM = 64
N_META = 16
EPS = 1e-6
POOL_WINDOWS = (2, 4, 8, 16)
POOL_GROUP = 64
LANES = 128
BF16_ROWS = 16
HALO = 16
PART_STRIDE = 8
LOG2E = math.log2(math.e)
NEG_BIG = -1e30
ZERO_EXP2 = 152.0
NORM_SLACK = 1.02
VMEM_LIMIT = 56 * 1024 * 1024
F32 = jnp.float32
BF16 = jnp.bfloat16

ATTN_TILE = 256
FOX_Q_BLOCKS = 3
ROW_TILE = 768
FFN_LAST_TILE = 1024
FF_CHUNKS = 11
WEIGHT_CAST_COLS = 384
WEIGHT_CAST_ROWS = 256
SB_WAVE = 1
SB_WAVE_DIAGONAL = 2
SB_SUM_PARTS = 1


def _dot(a, b):
    return jnp.dot(a, b, preferred_element_type=F32)


def _rms(x, gain):
    ms = jnp.mean(x * x, axis=-1, keepdims=True)
    return x * lax.rsqrt(ms + EPS) * gain


def _split3(x):
    hi = x.astype(BF16)
    r = x - hi.astype(F32)
    mid = r.astype(BF16)
    lo = (r - mid.astype(F32)).astype(BF16)
    return hi, mid, lo


def _pack_parts(x):
    hi, mid, lo = _split3(x)
    return (hi.astype(F32) + pltpu.roll(mid.astype(F32), PART_STRIDE, 1)
            + pltpu.roll(lo.astype(F32), 2 * PART_STRIDE, 1)).astype(BF16)


def _head_sq_norms(q, k):
    width = q.shape[1]
    sq = jnp.concatenate([(x.astype(F32) * x.astype(F32)).astype(BF16) for x in (q, k)], axis=1)
    d = lax.broadcasted_iota(jnp.int32, (2 * width, LANES), 0)
    lane = lax.broadcasted_iota(jnp.int32, (2 * width, LANES), 1)
    target = jnp.where(d < width, PART_STRIDE + d // HEAD_DIM, 2 * PART_STRIDE + (d - width) // HEAD_DIM)
    return _dot(sq, (lane == target).astype(BF16))


def _in_proj_kernel(h_ref, g_ref, w_ref, fb_ref, *outs_and_scratch, **static):
    _in_proj_tile(h_ref[0], g_ref, w_ref, fb_ref, *outs_and_scratch, **static)


def _embed_in_proj_kernel(x_ref, x_halo_ref, x_tail_ref, meta_ref, g_ref, w_ref, fb_ref, h_ref,
                          *outs_and_scratch, n_full, **static):
    i = pl.program_id(1)
    rows = h_ref.shape[1] - N_META
    tail = jnp.concatenate([x_tail_ref[0], jnp.zeros((rows - x_tail_ref.shape[1], h_ref.shape[2]), F32)],
                           axis=0)
    body = jnp.where(i < n_full, x_ref[0, :rows], jnp.where(i == n_full, tail, 0.0))
    head = jnp.where(i == 0, meta_ref[...], jnp.where(i <= n_full, x_halo_ref[0], 0.0))
    tile = jnp.concatenate([head, body], axis=0)
    h_ref[0] = tile
    _in_proj_tile(tile, g_ref, w_ref, fb_ref, *outs_and_scratch, **static)


def _in_proj_tile(h, g_ref, w_ref, fb_ref,
                  u_ref, qf_ref, kf_ref, vf_ref, cc_ref, stats_ref, qs_ref, ks_ref, vs_ref,
                  wb_ref, carry_ref, *, tm, width, n_heads):
    i = pl.program_id(1)
    pool_width = u_ref.shape[2]
    fox_cols = 3 * width + LANES

    @pl.when((pl.program_id(0) == 0) & (i == 0))
    def _():
        aligned = pool_width + 3 * width

        def copy(dst, src, cols):
            for c0 in range(0, cols, WEIGHT_CAST_COLS):
                n = min(WEIGHT_CAST_COLS, cols - c0)
                wb_ref[:, dst + c0:dst + c0 + n] = w_ref[0, :, src + c0:src + c0 + n].astype(BF16)

        copy(0, 0, aligned)
        forget = w_ref[0, :, aligned:aligned + LANES]
        lane = lax.broadcasted_iota(jnp.int32, forget.shape, 1)
        wb_ref[:, aligned:aligned + LANES] = jnp.where(lane < n_heads, forget, 0.0).astype(BF16)
        copy(aligned + LANES, aligned + n_heads, 3 * width)

    @pl.when(i == 0)
    def _():
        carry_ref[...] = jnp.zeros_like(carry_ref)

    a = _rms(h, g_ref[...]).astype(BF16)
    u_ref[0] = _dot(a, wb_ref[:, :pool_width])

    scale = HEAD_DIM ** -0.5
    pf = _dot(a, wb_ref[:, pool_width:pool_width + fox_cols])
    qf = pf[:, :width] * (scale * LOG2E)
    kf = pf[:, width:2 * width].astype(BF16)
    qf_ref[0] = qf.T.astype(BF16)
    kf_ref[0] = kf
    vf_ref[0] = pf[:, 2 * width:3 * width].T.astype(BF16)
    ps = _dot(a, wb_ref[:, pool_width + fox_cols:])
    qs_ref[0] = (ps[:, :width] * (scale * LOG2E)).T.astype(BF16)
    ks_ref[0] = ps[:, width:2 * width].astype(BF16)
    vs_ref[0] = ps[:, 2 * width:].T.astype(BF16)

    fl = pf[:, 3 * width:] + fb_ref[...]
    lane = lax.broadcasted_iota(jnp.int32, fl.shape, 1)
    log_f = jnp.where(lane < n_heads,
                      (jnp.minimum(fl, 0.0) - jnp.log1p(jnp.exp(-jnp.abs(fl)))) * LOG2E, 0.0)
    row = lax.broadcasted_iota(jnp.int32, (tm, tm), 0)
    col = lax.broadcasted_iota(jnp.int32, (tm, tm), 1)
    sums = _dot((col <= row).astype(BF16), _pack_parts(log_f))
    sums = sums + pltpu.roll(sums, LANES - PART_STRIDE, 1) + pltpu.roll(sums, LANES - 2 * PART_STRIDE, 1)
    c = jnp.where(lane < PART_STRIDE, sums, 0.0) + carry_ref[:1, :]
    carry_ref[...] = jnp.broadcast_to(c[tm - 1:tm, :], carry_ref.shape)
    cc_ref[0] = _pack_parts(c)
    stats_ref[0] = c + _head_sq_norms(qf.astype(BF16), kf)


def _in_proj(h, g, w_all, layer, fb, *, tm, pool_width, width, n_heads, embed=None):
    if embed is None:
        b, lp, d = h.shape
    else:
        x, meta, lp = embed
        b, s_len, d = x.shape
    bf16_cols = pool_width + 3 * width + LANES + 3 * width
    row3 = lambda w: pl.BlockSpec((1, tm, w), lambda bi, i: (bi, i, 0))
    col3 = lambda w: pl.BlockSpec((1, w, tm), lambda bi, i: (bi, 0, i))
    full = lambda arr: pl.BlockSpec(arr.shape, lambda bi, i: (0,) * arr.ndim)
    layer_weight = pl.BlockSpec((1,) + w_all.shape[1:], lambda bi, i: (layer, 0, 0),
                                pipeline_mode=pl.Buffered(1))
    rows = jax.ShapeDtypeStruct((b, lp, width), BF16)
    cols = jax.ShapeDtypeStruct((b, width, lp), BF16)
    static = dict(tm=tm, width=width, n_heads=n_heads)
    in_specs = [full(g), layer_weight, full(fb)]
    out_specs = [row3(pool_width), col3(width), row3(width), col3(width), row3(LANES), row3(LANES),
                 col3(width), row3(width), col3(width)]
    out_shape = [jax.ShapeDtypeStruct((b, lp, pool_width), F32), cols, rows, cols,
                 jax.ShapeDtypeStruct((b, lp, LANES), BF16),
                 jax.ShapeDtypeStruct((b, lp, LANES), F32), cols, rows, cols]
    if embed is None:
        body, operands = functools.partial(_in_proj_kernel, **static), (h,)
        in_specs = [row3(d)] + in_specs
    else:
        n_full = s_len // tm
        rem = s_len - n_full * tm
        assert N_META == HALO and rem % 8 == 0 and 0 < rem <= tm - N_META and s_len % N_META == 0
        last_halo = s_len // N_META - 1
        x_specs = [
            pl.BlockSpec((1, tm, d), lambda bi, i: (bi, jnp.minimum(i, n_full - 1), 0)),
            pl.BlockSpec((1, N_META, d),
                         lambda bi, i: (bi, jnp.clip(i * (tm // N_META) - 1, 0, last_halo), 0)),
            pl.BlockSpec((pl.Element(1), pl.Element(rem), pl.Element(d)),
                         lambda bi, i: (bi, n_full * tm, 0)),
        ]
        body = functools.partial(_embed_in_proj_kernel, n_full=n_full, **static)
        operands = (x, x, x, meta)
        in_specs = x_specs + [full(meta)] + in_specs
        out_specs = [row3(d)] + out_specs
        out_shape = [jax.ShapeDtypeStruct((b, lp, d), F32)] + out_shape
    return pl.pallas_call(
        body,
        grid=(b, lp // tm),
        in_specs=in_specs,
        out_specs=out_specs,
        out_shape=out_shape,
        scratch_shapes=[pltpu.VMEM((d, bf16_cols), BF16), pltpu.VMEM((8, LANES), F32)],
        compiler_params=pltpu.CompilerParams(
            dimension_semantics=("arbitrary", "arbitrary"), vmem_limit_bytes=VMEM_LIMIT),
        name="in_proj",
    )(*operands, g, w_all, fb)


def _fox_plan_kernel(stats_ref, js_ref, *, tq, tk):
    s = stats_ref[0]
    lp = s.shape[0]
    rows_out = js_ref.shape[1]
    lane = lax.broadcasted_iota(jnp.int32, s.shape, 1)
    k_max_sq = pltpu.roll(jnp.max(s, axis=0, keepdims=True), LANES - 2 * PART_STRIDE, 1)
    q_sq = pltpu.roll(s, LANES - PART_STRIDE, 1)
    spread = jnp.where(lane < PART_STRIDE, jnp.sqrt(q_sq * k_max_sq), 0.0)
    reach = s + (2.0 * NORM_SLACK) * spread + ZERO_EXP2
    tiles = [jnp.max(reach[i * tq:(i + 1) * tq], axis=0, keepdims=True) for i in range(lp // tq)]
    tiles.append(jnp.full((rows_out - len(tiles), LANES), -NEG_BIG, F32))
    reach_max = jnp.concatenate(tiles, axis=0)
    n_key_tiles = lp // tk
    c_end = stats_ref[0, pl.ds(tk - 1, n_key_tiles, stride=tk), :]
    count = jnp.zeros(reach_max.shape, jnp.int32)
    for j in range(n_key_tiles):
        count = count + (c_end[j:j + 1, :] > reach_max).astype(jnp.int32)
    js_ref[0] = jnp.minimum(count, pltpu.roll(count, LANES - 1, 1))


def _fox_plan(stats, *, tq, tk):
    b, lp, _ = stats.shape
    rows_out = -(-(lp // tq) // 8) * 8
    return pl.pallas_call(
        functools.partial(_fox_plan_kernel, tq=tq, tk=tk),
        grid=(b,),
        in_specs=[pl.BlockSpec((1, lp, LANES), lambda bi: (bi, 0, 0))],
        out_specs=pl.BlockSpec((1, rows_out, LANES), lambda bi: (bi, 0, 0)),
        out_shape=jax.ShapeDtypeStruct((b, rows_out, LANES), jnp.int32),
        compiler_params=pltpu.CompilerParams(
            dimension_semantics=("parallel",), vmem_limit_bytes=VMEM_LIMIT),
        name="fox_plan",
    )(stats)


def _head_norm_t(o_t):
    ms = jnp.mean(o_t * o_t, axis=0, keepdims=True)
    return o_t * lax.rsqrt(ms + EPS)


def _fox_kernel(first_ref, qt_ref, k_ref, vt_ref, cc_ref, g_ref, o_ref,
                qa_ref, m_ref, alpha_ref, pt_ref, st_ref, acc_ref, *, tq, tk):
    p = pl.program_id(1)
    qi = pl.program_id(2)
    nb = tq // tk
    n_full = nb * qi
    step = (pl.program_id(0) * pl.num_programs(1) + p) * pl.num_programs(2) + qi
    first = jnp.minimum(first_ref[step], n_full)

    qt = qt_ref[0]
    r = lax.broadcasted_iota(jnp.int32, qt.shape, 0)
    zero = jnp.zeros_like(qt)
    for hh in (0, 1):
        own = (r < HEAD_DIM) if hh == 0 else (r >= HEAD_DIM)
        h = 2 * p + hh
        picks = (r == h) | (r == h + PART_STRIDE) | (r == h + 2 * PART_STRIDE)
        qa_ref[hh, :LANES, :] = jnp.where(own, qt, zero)
        qa_ref[hh, LANES:, :] = jnp.where(picks, -1.0, 0.0).astype(BF16)
    m_ref[...] = jnp.full(m_ref.shape, NEG_BIG, F32)
    acc_ref[...] = jnp.zeros(acc_ref.shape, F32)
    ones = jnp.ones((BF16_ROWS, tk), BF16)

    def pieces_from(c0):
        return [(hh, c) for hh in (0, 1) for c in range(c0, tq, tk)]

    def scores(j, pieces):
        k0 = pl.multiple_of(j * tk, tk)
        kk = jnp.concatenate([k_ref[0, pl.ds(k0, tk), :], cc_ref[0, pl.ds(k0, tk), :]], axis=1)
        return [_dot(kk, qa_ref[hh, :, c:c + tk]) for hh, c in pieces]

    def softmax(st, pieces, masked):
        if masked:
            row = lax.broadcasted_iota(jnp.int32, (tk, tk), 0)
            col = lax.broadcasted_iota(jnp.int32, (tk, tk), 1)
            c0 = pieces[0][1]
            st = [jnp.where(row <= col, s, NEG_BIG) if c == c0 else s for s, (_, c) in zip(st, pieces)]
        weights = []
        for s, (hh, c) in zip(st, pieces):
            m_old = m_ref[hh, :, c:c + tk]
            m_new = jnp.maximum(m_old, jnp.max(s, axis=0, keepdims=True))
            m_ref[hh, :, c:c + tk] = m_new
            weights.append((jnp.exp2(s - m_new).astype(BF16), jnp.exp2(m_old - m_new)))
        return weights

    def accumulate(j, pieces, weights):
        k0 = pl.multiple_of(j * tk, tk)
        for (hh, c), (pt, alpha) in zip(pieces, weights):
            vt = jnp.concatenate(
                [vt_ref[0, hh * HEAD_DIM:(hh + 1) * HEAD_DIM, pl.ds(k0, tk)], ones], axis=0)
            acc_ref[hh, :, c:c + tk] = alpha * acc_ref[hh, :, c:c + tk] + _dot(vt, pt)

    every = pieces_from(0)
    alpha_ref[...] = jnp.ones(alpha_ref.shape, F32)
    pt_ref[...] = jnp.zeros(pt_ref.shape, BF16)

    def put_scores(st):
        for s, (hh, c) in zip(st, every):
            st_ref[hh, :, c:c + tk] = s

    def get_scores():
        return [st_ref[hh, :, c:c + tk] for hh, c in every]

    def put_weights(weights):
        for (pt, alpha), (hh, c) in zip(weights, every):
            pt_ref[hh, :, c:c + tk] = pt
            alpha_ref[hh, :, c:c + tk] = alpha

    def get_weights():
        return [(pt_ref[hh, :, c:c + tk], alpha_ref[hh, :, c:c + tk]) for hh, c in every]

    def full_tile(j, carry):
        ahead = scores(j + 1, every)
        accumulate(jnp.maximum(j - 1, 0), every, get_weights())
        put_weights(softmax(get_scores(), every, False))
        put_scores(ahead)
        return carry

    put_scores(scores(first, every))
    lax.fori_loop(first, n_full, full_tile, 0)
    diagonal = [pieces_from(d * tk) for d in range(nb)]
    st = [get_scores()] + [scores(n_full + d, diagonal[d]) for d in range(1, nb)]
    accumulate(jnp.maximum(n_full - 1, 0), every, get_weights())
    for d in range(nb):
        accumulate(n_full + d, diagonal[d], softmax(st[d], diagonal[d], True))

    out = []
    for hh in (0, 1):
        acc = acc_ref[hh]
        out.append(_head_norm_t(acc[:HEAD_DIM] / acc[HEAD_DIM:HEAD_DIM + 1]))
    o_ref[0] = (jnp.concatenate(out, axis=0).T * g_ref[...]).astype(o_ref.dtype)


def _fox_attention(first, qt, k, vt, cc, gain, *, tq, tk):
    b, lp, width = k.shape
    pairs = width // LANES
    grid_spec = pltpu.PrefetchScalarGridSpec(
        num_scalar_prefetch=1,
        grid=(b, pairs, lp // tq),
        in_specs=[pl.BlockSpec((1, LANES, tq), lambda bi, p, qi, first: (bi, p, qi)),
                  pl.BlockSpec((1, lp, LANES), lambda bi, p, qi, first: (bi, 0, p)),
                  pl.BlockSpec((1, LANES, lp), lambda bi, p, qi, first: (bi, p, 0)),
                  pl.BlockSpec((1, lp, LANES), lambda bi, p, qi, first: (bi, 0, 0)),
                  pl.BlockSpec((1, LANES), lambda bi, p, qi, first: (0, p))],
        out_specs=pl.BlockSpec((1, tq, LANES), lambda bi, p, qi, first: (bi, qi, p)),
        scratch_shapes=[pltpu.VMEM((2, 2 * LANES, tq), BF16),
                        pltpu.VMEM((2, 1, tq), F32),
                        pltpu.VMEM((2, 1, tq), F32),
                        pltpu.VMEM((2, tk, tq), BF16),
                        pltpu.VMEM((2, tk, tq), F32),
                        pltpu.VMEM((2, HEAD_DIM + BF16_ROWS, tq), F32)])
    return pl.pallas_call(
        functools.partial(_fox_kernel, tq=tq, tk=tk),
        grid_spec=grid_spec,
        out_shape=jax.ShapeDtypeStruct((b, lp, width), BF16),
        compiler_params=pltpu.CompilerParams(
            dimension_semantics=("parallel", "parallel", "parallel"), vmem_limit_bytes=VMEM_LIMIT),
        name="fox_attention",
    )(first, qt, k, vt, cc, gain)


def _sb_kernel(qt_ref, k_ref, vt_ref, tri_ref, g_ref, o_ref, qa_ref, tail_ref, acc_ref,
               *, t, n_heads, group):
    gi = pl.program_id(1)
    pieces = [(a, h) for a in range(group) for h in range(n_heads)]
    for a, h in pieces:
        qt = qt_ref[0, h * HEAD_DIM:(h + 1) * HEAD_DIM, a * t:(a + 1) * t]
        zero = jnp.zeros_like(qt)
        qa_ref[a, h] = jnp.concatenate([qt, zero] if h % 2 == 0 else [zero, qt], axis=0)

    def key_tiles(back, masked):
        tri = tri_ref[...]
        own = [gi * group + a for a in range(group)]
        exists = [own[a] >= back for a in range(group)]
        k0 = [pl.multiple_of(jnp.maximum(own[a] - back, 0) * t, t) for a in range(group)]
        if masked:
            row = lax.broadcasted_iota(jnp.int32, (t, t), 0)
            col = lax.broadcasted_iota(jnp.int32, (t, t), 1)
            strict = row < col
        z, split, cum, weights, scale = {}, {}, {}, {}, {}

        def logits(a, h):
            z[a, h] = _dot(k_ref[0, pl.ds(k0[a], t), (h // 2) * LANES:(h // 2 + 1) * LANES], qa_ref[a, h])

        def softplus_parts(a, h):
            sp = jnp.maximum(z[a, h], 0.0) + jnp.log2(1.0 + jnp.exp2(-jnp.abs(z[a, h])))
            if masked:
                sp = jnp.where(strict, sp, 0.0)
            parts = [sp.astype(BF16)]
            for _ in range(SB_SUM_PARTS - 1):
                sp = sp - parts[-1].astype(F32)
                parts.append(sp.astype(BF16))
            split[a, h] = jnp.concatenate(parts, axis=0)

        def suffix_sums(a, h):
            cum[a, h] = _dot(tri, split.pop((a, h)))

        def stick_weights(a, h):
            c = cum.pop((a, h))
            e = z.pop((a, h)) - c
            if masked:
                e = jnp.where(strict, e, NEG_BIG)
            weights[a, h] = jnp.exp2(e).astype(BF16)
            if masked:
                tail_ref[a, h] = -c[:1, :]
            else:
                tail = jnp.where(exists[a], tail_ref[a, h], NEG_BIG)
                scale[a, h] = jnp.exp2(tail)
                tail_ref[a, h] = tail - c[:1, :]

        def accumulate(a, h):
            vt = vt_ref[0, h * HEAD_DIM:(h + 1) * HEAD_DIM, pl.ds(k0[a], t)]
            pv = _dot(vt, weights.pop((a, h)))
            acc_ref[a, h] = pv if masked else acc_ref[a, h] + scale.pop((a, h)) * pv

        stages = (logits, softplus_parts, suffix_sums, stick_weights, accumulate)
        wave = SB_WAVE_DIAGONAL if masked else SB_WAVE
        waves = [pieces[w:w + wave] for w in range(0, len(pieces), wave)]
        for step in range(len(waves) + len(stages) - 1):
            for s in reversed(range(len(stages))):
                if 0 <= step - s < len(waves):
                    for a, h in waves[step - s]:
                        stages[s](a, h)

    def live():
        return jnp.max(tail_ref[...]) > -ZERO_EXP2

    key_tiles(0, True)

    def cond(carry):
        back, go = carry
        return jnp.logical_and(back <= gi * group + (group - 1), go)

    def body(carry):
        back, _ = carry
        key_tiles(back, False)
        return back + 1, live()

    lax.while_loop(cond, body, (1, live()))
    for a in range(group):
        out = jnp.concatenate([_head_norm_t(acc_ref[a, h]) for h in range(n_heads)], axis=0)
        o_ref[0, a * t:(a + 1) * t, :] = (out.T * g_ref[...]).astype(o_ref.dtype)


def _sb_attention(qt, k, vt, gain, *, t, group):
    b, lp, width = k.shape
    n_heads = width // HEAD_DIM
    idx = jnp.arange(t)
    tri = (idx[None, :] >= idx[:, None]).astype(BF16)
    tri = jnp.concatenate([tri] * SB_SUM_PARTS, axis=1)
    return pl.pallas_call(
        functools.partial(_sb_kernel, t=t, n_heads=n_heads, group=group),
        grid=(b, lp // (group * t)),
        in_specs=[pl.BlockSpec((1, width, group * t), lambda bi, gi: (bi, 0, gi)),
                  pl.BlockSpec((1, lp, width), lambda bi, gi: (bi, 0, 0)),
                  pl.BlockSpec((1, width, lp), lambda bi, gi: (bi, 0, 0)),
                  pl.BlockSpec((t, SB_SUM_PARTS * t), lambda bi, gi: (0, 0)),
                  pl.BlockSpec((1, width), lambda bi, gi: (0, 0))],
        out_specs=pl.BlockSpec((1, group * t, width), lambda bi, gi: (bi, gi, 0)),
        out_shape=jax.ShapeDtypeStruct((b, lp, width), BF16),
        scratch_shapes=[pltpu.VMEM((group, n_heads, LANES, t), BF16),
                        pltpu.VMEM((group, n_heads, 1, t), F32),
                        pltpu.VMEM((group, n_heads, HEAD_DIM, t), F32)],
        compiler_params=pltpu.CompilerParams(
            dimension_semantics=("parallel", "parallel"), vmem_limit_bytes=VMEM_LIMIT),
        name="sb_attention",
    )(qt, k, vt, tri, gain)


def _out_proj_kernel(h_ref, u_ref, up_ref, yf_ref, ys_ref, pw_ref, psc_ref, w_ref, o_ref, wo_ref, *, tm):
    i = pl.program_id(1)

    @pl.when((pl.program_id(0) == 0) & (i == 0))
    def _():
        for r0 in range(0, wo_ref.shape[0], WEIGHT_CAST_ROWS):
            wo_ref[r0:r0 + WEIGHT_CAST_ROWS, :] = w_ref[0, r0:r0 + WEIGHT_CAST_ROWS, :].astype(BF16)

    u = u_ref[0]
    pool_width = u.shape[1]
    out = h_ref[0] + _dot(jnp.concatenate([yf_ref[0], ys_ref[0]], axis=1), wo_ref[pool_width:, :])
    halo = jnp.where(i == 0, 0.0, up_ref[0])
    x = jnp.concatenate([halo, u], axis=0)
    sums = []
    shift = 1
    for _ in POOL_WINDOWS:
        x = x + pltpu.roll(x, shift, 0)
        sums.append(x[HALO:])
        shift *= 2
    group = lax.broadcasted_iota(jnp.int32, u.shape, 1) // POOL_GROUP
    window_sum = sums[-1]
    window = jnp.full(u.shape, POOL_WINDOWS[-1], jnp.int32)
    for g in range(len(POOL_WINDOWS) - 2, -1, -1):
        window_sum = jnp.where(group == g, sums[g], window_sum)
        window = jnp.where(group == g, POOL_WINDOWS[g], window)
    t1 = i * tm + lax.broadcasted_iota(jnp.int32, u.shape, 0) + 1
    count = jnp.minimum(t1, window).astype(F32)
    d = (window_sum / count - u).astype(BF16)
    y_pool = (_dot(d, pw_ref[...]) * psc_ref[...]).astype(BF16)
    o_ref[0] = out + _dot(y_pool, wo_ref[:pool_width, :])


def _out_proj(h, u, yf, ys, pw, psc, wo_all, layer, *, tm):
    b, lp, d = h.shape
    assert wo_all.shape[1] % WEIGHT_CAST_ROWS == 0
    layer_weight = pl.BlockSpec((1,) + wo_all.shape[1:], lambda bi, i: (layer, 0, 0),
                                pipeline_mode=pl.Buffered(1))
    row3 = lambda w: pl.BlockSpec((1, tm, w), lambda bi, i: (bi, i, 0))
    full = lambda arr: pl.BlockSpec(arr.shape, lambda bi, i: (0,) * arr.ndim)
    halo_spec = pl.BlockSpec((1, HALO, u.shape[2]),
                             lambda bi, i: (bi, jnp.maximum(i * (tm // HALO) - 1, 0), 0))
    return pl.pallas_call(
        functools.partial(_out_proj_kernel, tm=tm),
        grid=(b, lp // tm),
        in_specs=[row3(d), row3(u.shape[2]), halo_spec, row3(yf.shape[2]), row3(ys.shape[2]),
                  full(pw), full(psc), layer_weight],
        out_specs=row3(d),
        out_shape=jax.ShapeDtypeStruct((b, lp, d), F32),
        scratch_shapes=[pltpu.VMEM(wo_all.shape[1:], BF16)],
        compiler_params=pltpu.CompilerParams(
            dimension_semantics=("arbitrary", "arbitrary"), vmem_limit_bytes=VMEM_LIMIT),
        name="out_proj",
    )(h, u, u, yf, ys, pw, psc, wo_all)


def _ffn_kernel(h_ref, g_ref, wg_ref, wu_ref, wd_ref, fg_ref, o_ref, *, chunks, final_norm):
    h = h_ref[0]
    a = _rms(h, g_ref[...]).astype(BF16)
    cw = wg_ref.shape[2] // chunks
    out = h
    for c in range(chunks):
        gate = _dot(a, wg_ref[0, :, c * cw:(c + 1) * cw])
        up = _dot(a, wu_ref[0, :, c * cw:(c + 1) * cw])
        act = (gate * jax.nn.sigmoid(gate) * up).astype(BF16)
        out = out + _dot(act, wd_ref[0, c * cw:(c + 1) * cw, :])
    if final_norm:
        out = _rms(out, fg_ref[...])
    o_ref[0] = out


def _ffn(h, g, wg, wu, wd, layer, fg, *, tm, first_row, n_rows, final_norm):
    b, _, d = h.shape
    rows_in = pl.BlockSpec((pl.Element(1), pl.Element(tm), pl.Element(d)),
                           lambda bi, i: (bi, pl.multiple_of(first_row + i * tm, 8), 0))
    rows_out = pl.BlockSpec((1, tm, d), lambda bi, i: (bi, i, 0))
    resident = lambda arr: pl.BlockSpec(arr.shape, lambda bi, i: (0,) * arr.ndim,
                                        pipeline_mode=pl.Buffered(1))
    of_layer = lambda arr: pl.BlockSpec((1,) + arr.shape[1:], lambda bi, i: (layer, 0, 0),
                                        pipeline_mode=pl.Buffered(1))
    return pl.pallas_call(
        functools.partial(_ffn_kernel, chunks=FF_CHUNKS, final_norm=final_norm),
        grid=(b, n_rows // tm),
        in_specs=[rows_in, resident(g), of_layer(wg), of_layer(wu), of_layer(wd), resident(fg)],
        out_specs=rows_out,
        out_shape=jax.ShapeDtypeStruct((b, n_rows, d), F32),
        compiler_params=pltpu.CompilerParams(
            dimension_semantics=("parallel", "parallel"), vmem_limit_bytes=VMEM_LIMIT),
        name="ffn",
    )(h, g, wg, wu, wd, fg)


def _largest_tile(n, limit):
    return max(t for t in range(8, limit + 1, 8) if n % t == 0)


def _block_diag(pool_w):
    groups, cin, cout = pool_w.shape
    out = jnp.zeros((groups * cin, groups * cout), pool_w.dtype)
    for g in range(groups):
        out = out.at[g * cin:(g + 1) * cin, g * cout:(g + 1) * cout].set(pool_w[g])
    return out


def _trunk(x, meta_tokens, norm1, w_in, forget_bias, pool_w, pool_scale, fox_out_gain,
           sb_out_gain, w_out, norm2, w_gate, w_up, w_down, final_norm, *,
           attn_tile, fox_q_blocks, row_tile):
    b, s_len, d = x.shape
    depth = norm1.shape[0]
    l = N_META + s_len
    fox_tq = fox_q_blocks * attn_tile
    step = fox_tq * row_tile // math.gcd(fox_tq, row_tile)
    lp = -(-l // step) * step
    pool_width = pool_scale.shape[1]
    fox_w = fox_out_gain.shape[1]
    fox_heads = fox_w // HEAD_DIM
    assert fox_heads <= PART_STRIDE

    o0 = pool_width
    o1 = o0 + 3 * fox_w
    o2 = o1 + fox_heads
    for i in range(depth):
        fb = jnp.zeros((1, LANES), F32).at[0, :fox_heads].set(forget_bias[i].astype(F32))
        proj = dict(tm=row_tile, pool_width=pool_width, width=fox_w, n_heads=fox_heads)
        if i == 0:
            h, *outs = _in_proj(None, norm1[i][None], w_in.astype(F32), i, fb,
                                embed=(x.astype(F32), meta_tokens.astype(F32), lp), **proj)
        else:
            outs = _in_proj(h, norm1[i][None], w_in.astype(F32), i, fb, **proj)
        u, qft, kf, vft, cc, stats, qst, ks, vst = outs
        plan = _fox_plan(stats, tq=fox_tq, tk=attn_tile)
        first = plan[:, :lp // fox_tq, 0:fox_heads:2].transpose(0, 2, 1).reshape(-1)
        y_fox = _fox_attention(first, qft, kf, vft, cc, fox_out_gain[i][None], tq=fox_tq, tk=attn_tile)
        y_sb = _sb_attention(qst, ks, vst, sb_out_gain[i][None], t=attn_tile, group=fox_q_blocks)
        h = _out_proj(h, u, y_fox, y_sb, _block_diag(pool_w[i]).astype(BF16), pool_scale[i][None],
                      w_out.astype(F32), i, tm=row_tile)
        last = i == depth - 1
        rows = dict(tm=_largest_tile(s_len, FFN_LAST_TILE), first_row=N_META, n_rows=s_len) if last \
            else dict(tm=row_tile, first_row=0, n_rows=lp)
        h = _ffn(h, norm2[i][None], w_gate.astype(BF16), w_up.astype(BF16), w_down.astype(BF16), i,
                 final_norm[None], final_norm=last, **rows)
    return h


def kernel(x, meta_tokens, norm1, w_in, forget_bias, pool_w, pool_scale, fox_out_gain, sb_out_gain,
           w_out, norm2, w_gate, w_up, w_down, final_norm):
    return _trunk(x, meta_tokens, norm1, w_in, forget_bias, pool_w, pool_scale, fox_out_gain,
                  sb_out_gain, w_out, norm2, w_gate, w_up, w_down, final_norm,
                  attn_tile=ATTN_TILE, fox_q_blocks=FOX_Q_BLOCKS, row_tile=ROW_TILE)
```

```python
import functools
import math

import jax
import jax.numpy as jnp
from jax import lax
from jax.experimental import pallas as pl
from jax.experimental.pallas import tpu as pltpu

HEAD_DIM = 64
N_META = 16
EPS = 1e-6
POOL_WINDOWS = (2, 4, 8, 16)
POOL_GROUP = 64
LANES = 128
BF16_ROWS = 16
HALO = 16
PART_STRIDE = 8
LOG2E = math.log2(math.e)
NEG_BIG = -1e30
ZERO_EXP2 = 152.0
NORM_SLACK = 1.02
VMEM_LIMIT = 56 * 1024 * 1024
F32 = jnp.float32
BF16 = jnp.bfloat16

ATTN_TILE = 256
FOX_Q_BLOCKS = 3
ROW_TILE = 768
FFN_LAST_TILE = 1024
FF_CHUNKS = 11
WEIGHT_CAST_COLS = 384
WEIGHT_CAST_ROWS = 256
SB_WAVE = 1
SB_WAVE_DIAGONAL = 2
SB_SUM_PARTS = 1


def _dot(a, b):
    return jnp.dot(a, b, preferred_element_type=F32)


def _rms(x, gain):
    ms = jnp.mean(x * x, axis=-1, keepdims=True)
    return x * lax.rsqrt(ms + EPS) * gain


def _split3(x):
    hi = x.astype(BF16)
    r = x - hi.astype(F32)
    mid = r.astype(BF16)
    lo = (r - mid.astype(F32)).astype(BF16)
    return hi, mid, lo


def _pack_parts(x):
    hi, mid, lo = _split3(x)
    return (hi.astype(F32) + pltpu.roll(mid.astype(F32), PART_STRIDE, 1)
            + pltpu.roll(lo.astype(F32), 2 * PART_STRIDE, 1)).astype(BF16)


def _head_sq_norms(q, k):
    width = q.shape[1]
    sq = jnp.concatenate([(x.astype(F32) * x.astype(F32)).astype(BF16) for x in (q, k)], axis=1)
    d = lax.broadcasted_iota(jnp.int32, (2 * width, LANES), 0)
    lane = lax.broadcasted_iota(jnp.int32, (2 * width, LANES), 1)
    target = jnp.where(d < width, PART_STRIDE + d // HEAD_DIM, 2 * PART_STRIDE + (d - width) // HEAD_DIM)
    return _dot(sq, (lane == target).astype(BF16))


def _in_proj_kernel(h_ref, g_ref, w_ref, fb_ref, *outs_and_scratch, **static):
    _in_proj_tile(h_ref[0], g_ref, w_ref, fb_ref, *outs_and_scratch, **static)


def _embed_in_proj_kernel(x_ref, x_halo_ref, x_tail_ref, meta_ref, g_ref, w_ref, fb_ref, h_ref,
                          *outs_and_scratch, n_full, **static):
    i = pl.program_id(1)
    rows = h_ref.shape[1] - N_META
    tail = jnp.concatenate([x_tail_ref[0], jnp.zeros((rows - x_tail_ref.shape[1], h_ref.shape[2]), F32)],
                           axis=0)
    body = jnp.where(i < n_full, x_ref[0, :rows], jnp.where(i == n_full, tail, 0.0))
    head = jnp.where(i == 0, meta_ref[...], jnp.where(i <= n_full, x_halo_ref[0], 0.0))
    tile = jnp.concatenate([head, body], axis=0)
    h_ref[0] = tile
    _in_proj_tile(tile, g_ref, w_ref, fb_ref, *outs_and_scratch, **static)


def _in_proj_tile(h, g_ref, w_ref, fb_ref,
                  u_ref, qf_ref, kf_ref, vf_ref, cc_ref, stats_ref, qs_ref, ks_ref, vs_ref,
                  wb_ref, carry_ref, *, tm, width, n_heads):
    i = pl.program_id(1)
    pool_width = u_ref.shape[2]
    fox_cols = 3 * width + LANES

    @pl.when((pl.program_id(0) == 0) & (i == 0))
    def _():
        aligned = pool_width + 3 * width

        def copy(dst, src, cols):
            for c0 in range(0, cols, WEIGHT_CAST_COLS):
                n = min(WEIGHT_CAST_COLS, cols - c0)
                wb_ref[:, dst + c0:dst + c0 + n] = w_ref[0, :, src + c0:src + c0 + n].astype(BF16)

        copy(0, 0, aligned)
        forget = w_ref[0, :, aligned:aligned + LANES]
        lane = lax.broadcasted_iota(jnp.int32, forget.shape, 1)
        wb_ref[:, aligned:aligned + LANES] = jnp.where(lane < n_heads, forget, 0.0).astype(BF16)
        copy(aligned + LANES, aligned + n_heads, 3 * width)

    @pl.when(i == 0)
    def _():
        carry_ref[...] = jnp.zeros_like(carry_ref)

    a = _rms(h, g_ref[...]).astype(BF16)
    u_ref[0] = _dot(a, wb_ref[:, :pool_width])

    scale = HEAD_DIM ** -0.5
    pf = _dot(a, wb_ref[:, pool_width:pool_width + fox_cols])
    qf = pf[:, :width] * (scale * LOG2E)
    kf = pf[:, width:2 * width].astype(BF16)
    qf_ref[0] = qf.T.astype(BF16)
    kf_ref[0] = kf
    vf_ref[0] = pf[:, 2 * width:3 * width].T.astype(BF16)
    ps = _dot(a, wb_ref[:, pool_width + fox_cols:])
    qs_ref[0] = (ps[:, :width] * (scale * LOG2E)).T.astype(BF16)
    ks_ref[0] = ps[:, width:2 * width].astype(BF16)
    vs_ref[0] = ps[:, 2 * width:].T.astype(BF16)

    fl = pf[:, 3 * width:] + fb_ref[...]
    lane = lax.broadcasted_iota(jnp.int32, fl.shape, 1)
    log_f = jnp.where(lane < n_heads,
                      (jnp.minimum(fl, 0.0) - jnp.log1p(jnp.exp(-jnp.abs(fl)))) * LOG2E, 0.0)
    row = lax.broadcasted_iota(jnp.int32, (tm, tm), 0)
    col = lax.broadcasted_iota(jnp.int32, (tm, tm), 1)
    sums = _dot((col <= row).astype(BF16), _pack_parts(log_f))
    sums = sums + pltpu.roll(sums, LANES - PART_STRIDE, 1) + pltpu.roll(sums, LANES - 2 * PART_STRIDE, 1)
    c = jnp.where(lane < PART_STRIDE, sums, 0.0) + carry_ref[:1, :]
    carry_ref[...] = jnp.broadcast_to(c[tm - 1:tm, :], carry_ref.shape)
    cc_ref[0] = _pack_parts(c)
    stats_ref[0] = c + _head_sq_norms(qf.astype(BF16), kf)


def _in_proj(h, g, w_all, layer, fb, *, tm, pool_width, width, n_heads, embed=None):
    if embed is None:
        b, lp, d = h.shape
    else:
        x, meta, lp = embed
        b, s_len, d = x.shape
    bf16_cols = pool_width + 3 * width + LANES + 3 * width
    row3 = lambda w: pl.BlockSpec((1, tm, w), lambda bi, i: (bi, i, 0))
    col3 = lambda w: pl.BlockSpec((1, w, tm), lambda bi, i: (bi, 0, i))
    full = lambda arr: pl.BlockSpec(arr.shape, lambda bi, i: (0,) * arr.ndim)
    layer_weight = pl.BlockSpec((1,) + w_all.shape[1:], lambda bi, i: (layer, 0, 0),
                                pipeline_mode=pl.Buffered(1))
    rows = jax.ShapeDtypeStruct((b, lp, width), BF16)
    cols = jax.ShapeDtypeStruct((b, width, lp), BF16)
    static = dict(tm=tm, width=width, n_heads=n_heads)
    in_specs = [full(g), layer_weight, full(fb)]
    out_specs = [row3(pool_width), col3(width), row3(width), col3(width), row3(LANES), row3(LANES),
                 col3(width), row3(width), col3(width)]
    out_shape = [jax.ShapeDtypeStruct((b, lp, pool_width), F32), cols, rows, cols,
                 jax.ShapeDtypeStruct((b, lp, LANES), BF16),
                 jax.ShapeDtypeStruct((b, lp, LANES), F32), cols, rows, cols]
    if embed is None:
        body, operands = functools.partial(_in_proj_kernel, **static), (h,)
        in_specs = [row3(d)] + in_specs
    else:
        n_full = s_len // tm
        rem = s_len - n_full * tm
        assert N_META == HALO and rem % 8 == 0 and 0 < rem <= tm - N_META and s_len % N_META == 0
        last_halo = s_len // N_META - 1
        x_specs = [
            pl.BlockSpec((1, tm, d), lambda bi, i: (bi, jnp.minimum(i, n_full - 1), 0)),
            pl.BlockSpec((1, N_META, d),
                         lambda bi, i: (bi, jnp.clip(i * (tm // N_META) - 1, 0, last_halo), 0)),
            pl.BlockSpec((pl.Element(1), pl.Element(rem), pl.Element(d)),
                         lambda bi, i: (bi, n_full * tm, 0)),
        ]
        body = functools.partial(_embed_in_proj_kernel, n_full=n_full, **static)
        operands = (x, x, x, meta)
        in_specs = x_specs + [full(meta)] + in_specs
        out_specs = [row3(d)] + out_specs
        out_shape = [jax.ShapeDtypeStruct((b, lp, d), F32)] + out_shape
    return pl.pallas_call(
        body,
        grid=(b, lp // tm),
        in_specs=in_specs,
        out_specs=out_specs,
        out_shape=out_shape,
        scratch_shapes=[pltpu.VMEM((d, bf16_cols), BF16), pltpu.VMEM((8, LANES), F32)],
        compiler_params=pltpu.CompilerParams(
            dimension_semantics=("arbitrary", "arbitrary"), vmem_limit_bytes=VMEM_LIMIT),
        name="in_proj",
    )(*operands, g, w_all, fb)


def _fox_plan_kernel(stats_ref, js_ref, *, tq, tk):
    s = stats_ref[0]
    lp = s.shape[0]
    rows_out = js_ref.shape[1]
    lane = lax.broadcasted_iota(jnp.int32, s.shape, 1)
    k_max_sq = pltpu.roll(jnp.max(s, axis=0, keepdims=True), LANES - 2 * PART_STRIDE, 1)
    q_sq = pltpu.roll(s, LANES - PART_STRIDE, 1)
    spread = jnp.where(lane < PART_STRIDE, jnp.sqrt(q_sq * k_max_sq), 0.0)
    reach = s + (2.0 * NORM_SLACK) * spread + ZERO_EXP2
    tiles = [jnp.max(reach[i * tq:(i + 1) * tq], axis=0, keepdims=True) for i in range(lp // tq)]
    tiles.append(jnp.full((rows_out - len(tiles), LANES), -NEG_BIG, F32))
    reach_max = jnp.concatenate(tiles, axis=0)
    n_key_tiles = lp // tk
    c_end = stats_ref[0, pl.ds(tk - 1, n_key_tiles, stride=tk), :]
    count = jnp.zeros(reach_max.shape, jnp.int32)
    for j in range(n_key_tiles):
        count = count + (c_end[j:j + 1, :] > reach_max).astype(jnp.int32)
    js_ref[0] = jnp.minimum(count, pltpu.roll(count, LANES - 1, 1))


def _fox_plan(stats, *, tq, tk):
    b, lp, _ = stats.shape
    rows_out = -(-(lp // tq) // 8) * 8
    return pl.pallas_call(
        functools.partial(_fox_plan_kernel, tq=tq, tk=tk),
        grid=(b,),
        in_specs=[pl.BlockSpec((1, lp, LANES), lambda bi: (bi, 0, 0))],
        out_specs=pl.BlockSpec((1, rows_out, LANES), lambda bi: (bi, 0, 0)),
        out_shape=jax.ShapeDtypeStruct((b, rows_out, LANES), jnp.int32),
        compiler_params=pltpu.CompilerParams(
            dimension_semantics=("parallel",), vmem_limit_bytes=VMEM_LIMIT),
        name="fox_plan",
    )(stats)


def _head_norm_t(o_t):
    ms = jnp.mean(o_t * o_t, axis=0, keepdims=True)
    return o_t * lax.rsqrt(ms + EPS)


def _fox_kernel(first_ref, qt_ref, k_ref, vt_ref, cc_ref, g_ref, o_ref,
                qa_ref, m_ref, alpha_ref, pt_ref, st_ref, acc_ref, *, tq, tk):
    p = pl.program_id(1)
    qi = pl.program_id(2)
    nb = tq // tk
    n_full = nb * qi
    step = (pl.program_id(0) * pl.num_programs(1) + p) * pl.num_programs(2) + qi
    first = jnp.minimum(first_ref[step], n_full)

    qt = qt_ref[0]
    r = lax.broadcasted_iota(jnp.int32, qt.shape, 0)
    zero = jnp.zeros_like(qt)
    for hh in (0, 1):
        own = (r < HEAD_DIM) if hh == 0 else (r >= HEAD_DIM)
        h = 2 * p + hh
        picks = (r == h) | (r == h + PART_STRIDE) | (r == h + 2 * PART_STRIDE)
        qa_ref[hh, :LANES, :] = jnp.where(own, qt, zero)
        qa_ref[hh, LANES:, :] = jnp.where(picks, -1.0, 0.0).astype(BF16)
    m_ref[...] = jnp.full(m_ref.shape, NEG_BIG, F32)
    acc_ref[...] = jnp.zeros(acc_ref.shape, F32)
    ones = jnp.ones((BF16_ROWS, tk), BF16)

    def pieces_from(c0):
        return [(hh, c) for hh in (0, 1) for c in range(c0, tq, tk)]

    def scores(j, pieces):
        k0 = pl.multiple_of(j * tk, tk)
        kk = jnp.concatenate([k_ref[0, pl.ds(k0, tk), :], cc_ref[0, pl.ds(k0, tk), :]], axis=1)
        return [_dot(kk, qa_ref[hh, :, c:c + tk]) for hh, c in pieces]

    def softmax(st, pieces, masked):
        if masked:
            row = lax.broadcasted_iota(jnp.int32, (tk, tk), 0)
            col = lax.broadcasted_iota(jnp.int32, (tk, tk), 1)
            c0 = pieces[0][1]
            st = [jnp.where(row <= col, s, NEG_BIG) if c == c0 else s for s, (_, c) in zip(st, pieces)]
        weights = []
        for s, (hh, c) in zip(st, pieces):
            m_old = m_ref[hh, :, c:c + tk]
            m_new = jnp.maximum(m_old, jnp.max(s, axis=0, keepdims=True))
            m_ref[hh, :, c:c + tk] = m_new
            weights.append((jnp.exp2(s - m_new).astype(BF16), jnp.exp2(m_old - m_new)))
        return weights

    def accumulate(j, pieces, weights):
        k0 = pl.multiple_of(j * tk, tk)
        for (hh, c), (pt, alpha) in zip(pieces, weights):
            vt = jnp.concatenate(
                [vt_ref[0, hh * HEAD_DIM:(hh + 1) * HEAD_DIM, pl.ds(k0, tk)], ones], axis=0)
            acc_ref[hh, :, c:c + tk] = alpha * acc_ref[hh, :, c:c + tk] + _dot(vt, pt)

    every = pieces_from(0)
    alpha_ref[...] = jnp.ones(alpha_ref.shape, F32)
    pt_ref[...] = jnp.zeros(pt_ref.shape, BF16)

    def put_scores(st):
        for s, (hh, c) in zip(st, every):
            st_ref[hh, :, c:c + tk] = s

    def get_scores():
        return [st_ref[hh, :, c:c + tk] for hh, c in every]

    def put_weights(weights):
        for (pt, alpha), (hh, c) in zip(weights, every):
            pt_ref[hh, :, c:c + tk] = pt
            alpha_ref[hh, :, c:c + tk] = alpha

    def get_weights():
        return [(pt_ref[hh, :, c:c + tk], alpha_ref[hh, :, c:c + tk]) for hh, c in every]

    def full_tile(j, carry):
        ahead = scores(j + 1, every)
        accumulate(jnp.maximum(j - 1, 0), every, get_weights())
        put_weights(softmax(get_scores(), every, False))
        put_scores(ahead)
        return carry

    put_scores(scores(first, every))
    lax.fori_loop(first, n_full, full_tile, 0)
    diagonal = [pieces_from(d * tk) for d in range(nb)]
    st = [get_scores()] + [scores(n_full + d, diagonal[d]) for d in range(1, nb)]
    accumulate(jnp.maximum(n_full - 1, 0), every, get_weights())
    for d in range(nb):
        accumulate(n_full + d, diagonal[d], softmax(st[d], diagonal[d], True))

    out = []
    for hh in (0, 1):
        acc = acc_ref[hh]
        out.append(_head_norm_t(acc[:HEAD_DIM] / acc[HEAD_DIM:HEAD_DIM + 1]))
    o_ref[0] = (jnp.concatenate(out, axis=0).T * g_ref[...]).astype(o_ref.dtype)


def _fox_attention(first, qt, k, vt, cc, gain, *, tq, tk):
    b, lp, width = k.shape
    pairs = width // LANES
    grid_spec = pltpu.PrefetchScalarGridSpec(
        num_scalar_prefetch=1,
        grid=(b, pairs, lp // tq),
        in_specs=[pl.BlockSpec((1, LANES, tq), lambda bi, p, qi, first: (bi, p, qi)),
                  pl.BlockSpec((1, lp, LANES), lambda bi, p, qi, first: (bi, 0, p)),
                  pl.BlockSpec((1, LANES, lp), lambda bi, p, qi, first: (bi, p, 0)),
                  pl.BlockSpec((1, lp, LANES), lambda bi, p, qi, first: (bi, 0, 0)),
                  pl.BlockSpec((1, LANES), lambda bi, p, qi, first: (0, p))],
        out_specs=pl.BlockSpec((1, tq, LANES), lambda bi, p, qi, first: (bi, qi, p)),
        scratch_shapes=[pltpu.VMEM((2, 2 * LANES, tq), BF16),
                        pltpu.VMEM((2, 1, tq), F32),
                        pltpu.VMEM((2, 1, tq), F32),
                        pltpu.VMEM((2, tk, tq), BF16),
                        pltpu.VMEM((2, tk, tq), F32),
                        pltpu.VMEM((2, HEAD_DIM + BF16_ROWS, tq), F32)])
    return pl.pallas_call(
        functools.partial(_fox_kernel, tq=tq, tk=tk),
        grid_spec=grid_spec,
        out_shape=jax.ShapeDtypeStruct((b, lp, width), BF16),
        compiler_params=pltpu.CompilerParams(
            dimension_semantics=("parallel", "parallel", "parallel"), vmem_limit_bytes=VMEM_LIMIT),
        name="fox_attention",
    )(first, qt, k, vt, cc, gain)


def _sb_kernel(qt_ref, k_ref, vt_ref, tri_ref, g_ref, o_ref, qa_ref, tail_ref, acc_ref,
               *, t, n_heads, group):
    gi = pl.program_id(1)
    pieces = [(a, h) for a in range(group) for h in range(n_heads)]
    for a, h in pieces:
        qt = qt_ref[0, h * HEAD_DIM:(h + 1) * HEAD_DIM, a * t:(a + 1) * t]
        zero = jnp.zeros_like(qt)
        qa_ref[a, h] = jnp.concatenate([qt, zero] if h % 2 == 0 else [zero, qt], axis=0)

    def key_tiles(back, masked):
        tri = tri_ref[...]
        own = [gi * group + a for a in range(group)]
        exists = [own[a] >= back for a in range(group)]
        k0 = [pl.multiple_of(jnp.maximum(own[a] - back, 0) * t, t) for a in range(group)]
        if masked:
            row = lax.broadcasted_iota(jnp.int32, (t, t), 0)
            col = lax.broadcasted_iota(jnp.int32, (t, t), 1)
            strict = row < col
        z, split, cum, weights, scale = {}, {}, {}, {}, {}

        def logits(a, h):
            z[a, h] = _dot(k_ref[0, pl.ds(k0[a], t), (h // 2) * LANES:(h // 2 + 1) * LANES], qa_ref[a, h])

        def softplus_parts(a, h):
            sp = jnp.maximum(z[a, h], 0.0) + jnp.log2(1.0 + jnp.exp2(-jnp.abs(z[a, h])))
            if masked:
                sp = jnp.where(strict, sp, 0.0)
            parts = [sp.astype(BF16)]
            for _ in range(SB_SUM_PARTS - 1):
                sp = sp - parts[-1].astype(F32)
                parts.append(sp.astype(BF16))
            split[a, h] = jnp.concatenate(parts, axis=0)

        def suffix_sums(a, h):
            cum[a, h] = _dot(tri, split.pop((a, h)))

        def stick_weights(a, h):
            c = cum.pop((a, h))
            e = z.pop((a, h)) - c
            if masked:
                e = jnp.where(strict, e, NEG_BIG)
            weights[a, h] = jnp.exp2(e).astype(BF16)
            if masked:
                tail_ref[a, h] = -c[:1, :]
            else:
                tail = jnp.where(exists[a], tail_ref[a, h], NEG_BIG)
                scale[a, h] = jnp.exp2(tail)
                tail_ref[a, h] = tail - c[:1, :]

        def accumulate(a, h):
            vt = vt_ref[0, h * HEAD_DIM:(h + 1) * HEAD_DIM, pl.ds(k0[a], t)]
            pv = _dot(vt, weights.pop((a, h)))
            acc_ref[a, h] = pv if masked else acc_ref[a, h] + scale.pop((a, h)) * pv

        stages = (logits, softplus_parts, suffix_sums, stick_weights, accumulate)
        wave = SB_WAVE_DIAGONAL if masked else SB_WAVE
        waves = [pieces[w:w + wave] for w in range(0, len(pieces), wave)]
        for step in range(len(waves) + len(stages) - 1):
            for s in reversed(range(len(stages))):
                if 0 <= step - s < len(waves):
                    for a, h in waves[step - s]:
                        stages[s](a, h)

    def live():
        return jnp.max(tail_ref[...]) > -ZERO_EXP2

    key_tiles(0, True)

    def cond(carry):
        back, go = carry
        return jnp.logical_and(back <= gi * group + (group - 1), go)

    def body(carry):
        back, _ = carry
        key_tiles(back, False)
        return back + 1, live()

    lax.while_loop(cond, body, (1, live()))
    for a in range(group):
        out = jnp.concatenate([_head_norm_t(acc_ref[a, h]) for h in range(n_heads)], axis=0)
        o_ref[0, a * t:(a + 1) * t, :] = (out.T * g_ref[...]).astype(o_ref.dtype)


def _sb_attention(qt, k, vt, gain, *, t, group):
    b, lp, width = k.shape
    n_heads = width // HEAD_DIM
    idx = jnp.arange(t)
    tri = (idx[None, :] >= idx[:, None]).astype(BF16)
    tri = jnp.concatenate([tri] * SB_SUM_PARTS, axis=1)
    return pl.pallas_call(
        functools.partial(_sb_kernel, t=t, n_heads=n_heads, group=group),
        grid=(b, lp // (group * t)),
        in_specs=[pl.BlockSpec((1, width, group * t), lambda bi, gi: (bi, 0, gi)),
                  pl.BlockSpec((1, lp, width), lambda bi, gi: (bi, 0, 0)),
                  pl.BlockSpec((1, width, lp), lambda bi, gi: (bi, 0, 0)),
                  pl.BlockSpec((t, SB_SUM_PARTS * t), lambda bi, gi: (0, 0)),
                  pl.BlockSpec((1, width), lambda bi, gi: (0, 0))],
        out_specs=pl.BlockSpec((1, group * t, width), lambda bi, gi: (bi, gi, 0)),
        out_shape=jax.ShapeDtypeStruct((b, lp, width), BF16),
        scratch_shapes=[pltpu.VMEM((group, n_heads, LANES, t), BF16),
                        pltpu.VMEM((group, n_heads, 1, t), F32),
                        pltpu.VMEM((group, n_heads, HEAD_DIM, t), F32)],
        compiler_params=pltpu.CompilerParams(
            dimension_semantics=("parallel", "parallel"), vmem_limit_bytes=VMEM_LIMIT),
        name="sb_attention",
    )(qt, k, vt, tri, gain)


def _out_proj_kernel(u_ref, up_ref, yf_ref, ys_ref, pw_ref, psc_ref, w_ref, o_ref, wo_ref, *, tm):
    i = pl.program_id(1)

    @pl.when((pl.program_id(0) == 0) & (i == 0))
    def _():
        for r0 in range(0, wo_ref.shape[0], WEIGHT_CAST_ROWS):
            wo_ref[r0:r0 + WEIGHT_CAST_ROWS, :] = w_ref[0, r0:r0 + WEIGHT_CAST_ROWS, :].astype(BF16)

    u = u_ref[0]
    pool_width = u.shape[1]
    out = _dot(jnp.concatenate([yf_ref[0], ys_ref[0]], axis=1), wo_ref[pool_width:, :])
    halo = jnp.where(i == 0, 0.0, up_ref[0])
    x = jnp.concatenate([halo, u], axis=0)
    sums = []
    shift = 1
    for _ in POOL_WINDOWS:
        x = x + pltpu.roll(x, shift, 0)
        sums.append(x[HALO:])
        shift *= 2
    group = lax.broadcasted_iota(jnp.int32, u.shape, 1) // POOL_GROUP
    window_sum = sums[-1]
    window = jnp.full(u.shape, POOL_WINDOWS[-1], jnp.int32)
    for g in range(len(POOL_WINDOWS) - 2, -1, -1):
        window_sum = jnp.where(group == g, sums[g], window_sum)
        window = jnp.where(group == g, POOL_WINDOWS[g], window)
    t1 = i * tm + lax.broadcasted_iota(jnp.int32, u.shape, 0) + 1
    count = jnp.minimum(t1, window).astype(F32)
    d = (window_sum / count - u).astype(BF16)
    y_pool = (_dot(d, pw_ref[...]) * psc_ref[...]).astype(BF16)
    o_ref[0] = out + _dot(y_pool, wo_ref[:pool_width, :])


def _out_proj(u, yf, ys, pw, psc, wo_all, layer, *, tm):
    b, lp, _ = u.shape
    d = wo_all.shape[2]
    assert wo_all.shape[1] % WEIGHT_CAST_ROWS == 0
    layer_weight = pl.BlockSpec((1,) + wo_all.shape[1:], lambda bi, i: (layer, 0, 0),
                                pipeline_mode=pl.Buffered(1))
    row3 = lambda w: pl.BlockSpec((1, tm, w), lambda bi, i: (bi, i, 0))
    full = lambda arr: pl.BlockSpec(arr.shape, lambda bi, i: (0,) * arr.ndim)
    halo_spec = pl.BlockSpec((1, HALO, u.shape[2]),
                             lambda bi, i: (bi, jnp.maximum(i * (tm // HALO) - 1, 0), 0))
    return pl.pallas_call(
        functools.partial(_out_proj_kernel, tm=tm),
        grid=(b, lp // tm),
        in_specs=[row3(u.shape[2]), halo_spec, row3(yf.shape[2]), row3(ys.shape[2]),
                  full(pw), full(psc), layer_weight],
        out_specs=row3(d),
        out_shape=jax.ShapeDtypeStruct((b, lp, d), F32),
        scratch_shapes=[pltpu.VMEM(wo_all.shape[1:], BF16)],
        compiler_params=pltpu.CompilerParams(
            dimension_semantics=("arbitrary", "arbitrary"), vmem_limit_bytes=VMEM_LIMIT),
        name="out_proj",
    )(u, u, yf, ys, pw, psc, wo_all)


def _ffn_kernel(h_ref, dh_ref, g_ref, wg_ref, wu_ref, wd_ref, fg_ref, o_ref, *, chunks, final_norm):
    h = h_ref[0] + dh_ref[0]
    a = _rms(h, g_ref[...]).astype(BF16)
    cw = wg_ref.shape[2] // chunks
    out = h
    for c in range(chunks):
        gate = _dot(a, wg_ref[0, :, c * cw:(c + 1) * cw])
        up = _dot(a, wu_ref[0, :, c * cw:(c + 1) * cw])
        act = (gate * jax.nn.sigmoid(gate) * up).astype(BF16)
        out = out + _dot(act, wd_ref[0, c * cw:(c + 1) * cw, :])
    if final_norm:
        out = _rms(out, fg_ref[...])
    o_ref[0] = out


def _ffn(h, dh, g, wg, wu, wd, layer, fg, *, tm, first_row, n_rows, final_norm):
    b, _, d = h.shape
    rows_in = pl.BlockSpec((pl.Element(1), pl.Element(tm), pl.Element(d)),
                           lambda bi, i: (bi, pl.multiple_of(first_row + i * tm, 8), 0))
    rows_out = pl.BlockSpec((1, tm, d), lambda bi, i: (bi, i, 0))
    resident = lambda arr: pl.BlockSpec(arr.shape, lambda bi, i: (0,) * arr.ndim,
                                        pipeline_mode=pl.Buffered(1))
    of_layer = lambda arr: pl.BlockSpec((1,) + arr.shape[1:], lambda bi, i: (layer, 0, 0),
                                        pipeline_mode=pl.Buffered(1))
    return pl.pallas_call(
        functools.partial(_ffn_kernel, chunks=FF_CHUNKS, final_norm=final_norm),
        grid=(b, n_rows // tm),
        in_specs=[rows_in, rows_in, resident(g), of_layer(wg), of_layer(wu), of_layer(wd), resident(fg)],
        out_specs=rows_out,
        out_shape=jax.ShapeDtypeStruct((b, n_rows, d), F32),
        compiler_params=pltpu.CompilerParams(
            dimension_semantics=("parallel", "parallel"), vmem_limit_bytes=VMEM_LIMIT),
        name="ffn",
    )(h, dh, g, wg, wu, wd, fg)


def _largest_tile(n, limit):
    return max(t for t in range(8, limit + 1, 8) if n % t == 0)


def _block_diag(pool_w):
    groups, cin, cout = pool_w.shape
    out = jnp.zeros((groups * cin, groups * cout), pool_w.dtype)
    for g in range(groups):
        out = out.at[g * cin:(g + 1) * cin, g * cout:(g + 1) * cout].set(pool_w[g])
    return out


def _trunk(x, meta_tokens, norm1, w_in, forget_bias, pool_w, pool_scale, fox_out_gain,
           sb_out_gain, w_out, norm2, w_gate, w_up, w_down, final_norm, *,
           attn_tile, fox_q_blocks, row_tile):
    b, s_len, d = x.shape
    depth = norm1.shape[0]
    l = N_META + s_len
    fox_tq = fox_q_blocks * attn_tile
    step = fox_tq * row_tile // math.gcd(fox_tq, row_tile)
    lp = -(-l // step) * step
    pool_width = pool_scale.shape[1]
    fox_w = fox_out_gain.shape[1]
    fox_heads = fox_w // HEAD_DIM
    assert fox_heads <= PART_STRIDE

    o0 = pool_width
    o1 = o0 + 3 * fox_w
    o2 = o1 + fox_heads
    for i in range(depth):
        fb = jnp.zeros((1, LANES), F32).at[0, :fox_heads].set(forget_bias[i].astype(F32))
        proj = dict(tm=row_tile, pool_width=pool_width, width=fox_w, n_heads=fox_heads)
        if i == 0:
            h, *outs = _in_proj(None, norm1[i][None], w_in.astype(F32), i, fb,
                                embed=(x.astype(F32), meta_tokens.astype(F32), lp), **proj)
        else:
            outs = _in_proj(h, norm1[i][None], w_in.astype(F32), i, fb, **proj)
        u, qft, kf, vft, cc, stats, qst, ks, vst = outs
        plan = _fox_plan(stats, tq=fox_tq, tk=attn_tile)
        first = plan[:, :lp // fox_tq, 0:fox_heads:2].transpose(0, 2, 1).reshape(-1)
        y_fox = _fox_attention(first, qft, kf, vft, cc, fox_out_gain[i][None], tq=fox_tq, tk=attn_tile)
        y_sb = _sb_attention(qst, ks, vst, sb_out_gain[i][None], t=attn_tile, group=fox_q_blocks)
        dh = _out_proj(u, y_fox, y_sb, _block_diag(pool_w[i]).astype(BF16), pool_scale[i][None],
                       w_out.astype(F32), i, tm=row_tile)
        last = i == depth - 1
        rows = dict(tm=_largest_tile(s_len, FFN_LAST_TILE), first_row=N_META, n_rows=s_len) if last \
            else dict(tm=row_tile, first_row=0, n_rows=lp)
        h = _ffn(h, dh, norm2[i][None], w_gate.astype(BF16), w_up.astype(BF16), w_down.astype(BF16), i,
                 final_norm[None], final_norm=last, **rows)
    return h


def kernel(x, meta_tokens, norm1, w_in, forget_bias, pool_w, pool_scale, fox_out_gain, sb_out_gain,
           w_out, norm2, w_gate, w_up, w_down, final_norm):
    return _trunk(x, meta_tokens, norm1, w_in, forget_bias, pool_w, pool_scale, fox_out_gain,
                  sb_out_gain, w_out, norm2, w_gate, w_up, w_down, final_norm,
                  attn_tile=ATTN_TILE, fox_q_blocks=FOX_Q_BLOCKS, row_tile=ROW_TILE)
```

```python
import functools
import math

import jax
import jax.numpy as jnp
from jax import lax
from jax.experimental import pallas as pl
from jax.experimental.pallas import tpu as pltpu

HEAD_DIM = 64
N_META = 16
EPS = 1e-6
POOL_WINDOWS = (2, 4, 8, 16)
POOL_GROUP = 64
LANES = 128
BF16_ROWS = 16
HALO = 16
PART_STRIDE = 8
LOG2E = math.log2(math.e)
NEG_BIG = -1e30
ZERO_EXP2 = 152.0
NORM_SLACK = 1.02
VMEM_LIMIT = 56 * 1024 * 1024
F32 = jnp.float32
BF16 = jnp.bfloat16

ATTN_TILE = 256
FOX_Q_BLOCKS = 3
ROW_TILE = 768
FFN_LAST_TILE = 1024
FF_CHUNKS = 11
WEIGHT_CAST_COLS = 384
WEIGHT_CAST_ROWS = 256
SB_WAVE = 1
SB_WAVE_DIAGONAL = 2
SB_SUM_PARTS = 1


def _dot(a, b):
    return jnp.dot(a, b, preferred_element_type=F32)


def _rms(x, gain):
    ms = jnp.mean(x * x, axis=-1, keepdims=True)
    return x * lax.rsqrt(ms + EPS) * gain


def _split3(x):
    hi = x.astype(BF16)
    r = x - hi.astype(F32)
    mid = r.astype(BF16)
    lo = (r - mid.astype(F32)).astype(BF16)
    return hi, mid, lo


def _pack_parts(x):
    hi, mid, lo = _split3(x)
    return (hi.astype(F32) + pltpu.roll(mid.astype(F32), PART_STRIDE, 1)
            + pltpu.roll(lo.astype(F32), 2 * PART_STRIDE, 1)).astype(BF16)


def _head_sq_norms(q, k):
    width = q.shape[1]
    sq = jnp.concatenate([(x.astype(F32) * x.astype(F32)).astype(BF16) for x in (q, k)], axis=1)
    d = lax.broadcasted_iota(jnp.int32, (2 * width, LANES), 0)
    lane = lax.broadcasted_iota(jnp.int32, (2 * width, LANES), 1)
    target = jnp.where(d < width, PART_STRIDE + d // HEAD_DIM, 2 * PART_STRIDE + (d - width) // HEAD_DIM)
    return _dot(sq, (lane == target).astype(BF16))


def _in_proj_kernel(h_ref, g_ref, w_ref, fb_ref, *outs_and_scratch, **static):
    _in_proj_tile(h_ref[0], g_ref, w_ref, fb_ref, *outs_and_scratch, **static)


def _embed_in_proj_kernel(x_ref, x_halo_ref, x_tail_ref, meta_ref, g_ref, w_ref, fb_ref, h_ref,
                          *outs_and_scratch, n_full, **static):
    i = pl.program_id(1)
    rows = h_ref.shape[1] - N_META
    tail = jnp.concatenate([x_tail_ref[0], jnp.zeros((rows - x_tail_ref.shape[1], h_ref.shape[2]), F32)],
                           axis=0)
    body = jnp.where(i < n_full, x_ref[0, :rows], jnp.where(i == n_full, tail, 0.0))
    head = jnp.where(i == 0, meta_ref[...], jnp.where(i <= n_full, x_halo_ref[0], 0.0))
    tile = jnp.concatenate([head, body], axis=0)
    h_ref[0] = tile
    _in_proj_tile(tile, g_ref, w_ref, fb_ref, *outs_and_scratch, **static)


def _in_proj_tile(h, g_ref, w_ref, fb_ref,
                  u_ref, qf_ref, kf_ref, vf_ref, cc_ref, stats_ref, qs_ref, ks_ref, vs_ref,
                  wb_ref, carry_ref, *, tm, width, n_heads):
    i = pl.program_id(1)
    pool_width = u_ref.shape[2]
    fox_cols = 3 * width + LANES

    @pl.when((pl.program_id(0) == 0) & (i == 0))
    def _():
        aligned = pool_width + 3 * width

        def copy(dst, src, cols):
            for c0 in range(0, cols, WEIGHT_CAST_COLS):
                n = min(WEIGHT_CAST_COLS, cols - c0)
                wb_ref[:, dst + c0:dst + c0 + n] = w_ref[0, :, src + c0:src + c0 + n].astype(BF16)

        copy(0, 0, aligned)
        forget = w_ref[0, :, aligned:aligned + LANES]
        lane = lax.broadcasted_iota(jnp.int32, forget.shape, 1)
        wb_ref[:, aligned:aligned + LANES] = jnp.where(lane < n_heads, forget, 0.0).astype(BF16)
        copy(aligned + LANES, aligned + n_heads, 3 * width)

    @pl.when(i == 0)
    def _():
        carry_ref[...] = jnp.zeros_like(carry_ref)

    a = _rms(h, g_ref[...]).astype(BF16)
    u_ref[0] = _dot(a, wb_ref[:, :pool_width])

    scale = HEAD_DIM ** -0.5
    pf = _dot(a, wb_ref[:, pool_width:pool_width + fox_cols])
    qf = pf[:, :width] * (scale * LOG2E)
    kf = pf[:, width:2 * width].astype(BF16)
    qf_ref[0] = qf.T.astype(BF16)
    kf_ref[0] = kf
    vf_ref[0] = pf[:, 2 * width:3 * width].T.astype(BF16)
    ps = _dot(a, wb_ref[:, pool_width + fox_cols:])
    qs_ref[0] = (ps[:, :width] * (scale * LOG2E)).T.astype(BF16)
    ks_ref[0] = ps[:, width:2 * width].astype(BF16)
    vs_ref[0] = ps[:, 2 * width:].T.astype(BF16)

    fl = pf[:, 3 * width:] + fb_ref[...]
    lane = lax.broadcasted_iota(jnp.int32, fl.shape, 1)
    log_f = jnp.where(lane < n_heads,
                      (jnp.minimum(fl, 0.0) - jnp.log1p(jnp.exp(-jnp.abs(fl)))) * LOG2E, 0.0)
    row = lax.broadcasted_iota(jnp.int32, (tm, tm), 0)
    col = lax.broadcasted_iota(jnp.int32, (tm, tm), 1)
    sums = _dot((col <= row).astype(BF16), _pack_parts(log_f))
    sums = sums + pltpu.roll(sums, LANES - PART_STRIDE, 1) + pltpu.roll(sums, LANES - 2 * PART_STRIDE, 1)
    c = jnp.where(lane < PART_STRIDE, sums, 0.0) + carry_ref[:1, :]
    carry_ref[...] = jnp.broadcast_to(c[tm - 1:tm, :], carry_ref.shape)
    cc_ref[0] = _pack_parts(c)
    stats_ref[0] = c + _head_sq_norms(qf.astype(BF16), kf)


def _in_proj(h, g, w_all, layer, fb, *, tm, pool_width, width, n_heads, embed=None):
    if embed is None:
        b, lp, d = h.shape
    else:
        x, meta, lp = embed
        b, s_len, d = x.shape
    bf16_cols = pool_width + 3 * width + LANES + 3 * width
    row3 = lambda w: pl.BlockSpec((1, tm, w), lambda bi, i: (bi, i, 0))
    col3 = lambda w: pl.BlockSpec((1, w, tm), lambda bi, i: (bi, 0, i))
    full = lambda arr: pl.BlockSpec(arr.shape, lambda bi, i: (0,) * arr.ndim)
    layer_weight = pl.BlockSpec((1,) + w_all.shape[1:], lambda bi, i: (layer, 0, 0),
                                pipeline_mode=pl.Buffered(1))
    rows = jax.ShapeDtypeStruct((b, lp, width), BF16)
    cols = jax.ShapeDtypeStruct((b, width, lp), BF16)
    static = dict(tm=tm, width=width, n_heads=n_heads)
    in_specs = [full(g), layer_weight, full(fb)]
    out_specs = [row3(pool_width), col3(width), row3(width), col3(width), row3(LANES), row3(LANES),
                 col3(width), row3(width), col3(width)]
    out_shape = [jax.ShapeDtypeStruct((b, lp, pool_width), F32), cols, rows, cols,
                 jax.ShapeDtypeStruct((b, lp, LANES), BF16),
                 jax.ShapeDtypeStruct((b, lp, LANES), F32), cols, rows, cols]
    if embed is None:
        body, operands = functools.partial(_in_proj_kernel, **static), (h,)
        in_specs = [row3(d)] + in_specs
    else:
        n_full = s_len // tm
        rem = s_len - n_full * tm
        assert N_META == HALO and rem % 8 == 0 and 0 < rem <= tm - N_META and s_len % N_META == 0
        last_halo = s_len // N_META - 1
        x_specs = [
            pl.BlockSpec((1, tm, d), lambda bi, i: (bi, jnp.minimum(i, n_full - 1), 0)),
            pl.BlockSpec((1, N_META, d),
                         lambda bi, i: (bi, jnp.clip(i * (tm // N_META) - 1, 0, last_halo), 0)),
            pl.BlockSpec((pl.Element(1), pl.Element(rem), pl.Element(d)),
                         lambda bi, i: (bi, n_full * tm, 0)),
        ]
        body = functools.partial(_embed_in_proj_kernel, n_full=n_full, **static)
        operands = (x, x, x, meta)
        in_specs = x_specs + [full(meta)] + in_specs
        out_specs = [row3(d)] + out_specs
        out_shape = [jax.ShapeDtypeStruct((b, lp, d), F32)] + out_shape
    return pl.pallas_call(
        body,
        grid=(b, lp // tm),
        in_specs=in_specs,
        out_specs=out_specs,
        out_shape=out_shape,
        scratch_shapes=[pltpu.VMEM((d, bf16_cols), BF16), pltpu.VMEM((8, LANES), F32)],
        compiler_params=pltpu.CompilerParams(
            dimension_semantics=("arbitrary", "arbitrary"), vmem_limit_bytes=VMEM_LIMIT),
        name="in_proj",
    )(*operands, g, w_all, fb)


def _fox_plan_kernel(stats_ref, js_ref, *, tq, tk):
    s = stats_ref[0]
    lp = s.shape[0]
    rows_out = js_ref.shape[1]
    lane = lax.broadcasted_iota(jnp.int32, s.shape, 1)
    k_max_sq = pltpu.roll(jnp.max(s, axis=0, keepdims=True), LANES - 2 * PART_STRIDE, 1)
    q_sq = pltpu.roll(s, LANES - PART_STRIDE, 1)
    spread = jnp.where(lane < PART_STRIDE, jnp.sqrt(q_sq * k_max_sq), 0.0)
    reach = s + (2.0 * NORM_SLACK) * spread + ZERO_EXP2
    tiles = [jnp.max(reach[i * tq:(i + 1) * tq], axis=0, keepdims=True) for i in range(lp // tq)]
    tiles.append(jnp.full((rows_out - len(tiles), LANES), -NEG_BIG, F32))
    reach_max = jnp.concatenate(tiles, axis=0)
    n_key_tiles = lp // tk
    c_end = stats_ref[0, pl.ds(tk - 1, n_key_tiles, stride=tk), :]
    count = jnp.zeros(reach_max.shape, jnp.int32)
    for j in range(n_key_tiles):
        count = count + (c_end[j:j + 1, :] > reach_max).astype(jnp.int32)
    js_ref[0] = jnp.minimum(count, pltpu.roll(count, LANES - 1, 1))


def _fox_plan(stats, *, tq, tk):
    b, lp, _ = stats.shape
    rows_out = -(-(lp // tq) // 8) * 8
    return pl.pallas_call(
        functools.partial(_fox_plan_kernel, tq=tq, tk=tk),
        grid=(b,),
        in_specs=[pl.BlockSpec((1, lp, LANES), lambda bi: (bi, 0, 0))],
        out_specs=pl.BlockSpec((1, rows_out, LANES), lambda bi: (bi, 0, 0)),
        out_shape=jax.ShapeDtypeStruct((b, rows_out, LANES), jnp.int32),
        compiler_params=pltpu.CompilerParams(
            dimension_semantics=("parallel",), vmem_limit_bytes=VMEM_LIMIT),
        name="fox_plan",
    )(stats)


def _head_norm_t(o_t):
    ms = jnp.mean(o_t * o_t, axis=0, keepdims=True)
    return o_t * lax.rsqrt(ms + EPS)


def _fox_kernel(first_ref, qt_ref, k_ref, vt_ref, cc_ref, g_ref, o_ref,
                qa_ref, m_ref, alpha_ref, pt_ref, st_ref, acc_ref, *, tq, tk):
    p = pl.program_id(1)
    qi = pl.program_id(2)
    nb = tq // tk
    n_full = nb * qi
    step = (pl.program_id(0) * pl.num_programs(1) + p) * pl.num_programs(2) + qi
    first = jnp.minimum(first_ref[step], n_full)

    qt = qt_ref[0]
    r = lax.broadcasted_iota(jnp.int32, qt.shape, 0)
    zero = jnp.zeros_like(qt)
    for hh in (0, 1):
        own = (r < HEAD_DIM) if hh == 0 else (r >= HEAD_DIM)
        h = 2 * p + hh
        picks = (r == h) | (r == h + PART_STRIDE) | (r == h + 2 * PART_STRIDE)
        qa_ref[hh, :LANES, :] = jnp.where(own, qt, zero)
        qa_ref[hh, LANES:, :] = jnp.where(picks, -1.0, 0.0).astype(BF16)
    m_ref[...] = jnp.full(m_ref.shape, NEG_BIG, F32)
    acc_ref[...] = jnp.zeros(acc_ref.shape, F32)
    ones = jnp.ones((BF16_ROWS, tk), BF16)

    def pieces_from(c0):
        return [(hh, c) for hh in (0, 1) for c in range(c0, tq, tk)]

    def scores(j, pieces):
        k0 = pl.multiple_of(j * tk, tk)
        kk = jnp.concatenate([k_ref[0, pl.ds(k0, tk), :], cc_ref[0, pl.ds(k0, tk), :]], axis=1)
        return [_dot(kk, qa_ref[hh, :, c:c + tk]) for hh, c in pieces]

    def softmax(st, pieces, masked):
        if masked:
            row = lax.broadcasted_iota(jnp.int32, (tk, tk), 0)
            col = lax.broadcasted_iota(jnp.int32, (tk, tk), 1)
            c0 = pieces[0][1]
            st = [jnp.where(row <= col, s, NEG_BIG) if c == c0 else s for s, (_, c) in zip(st, pieces)]
        weights = []
        for s, (hh, c) in zip(st, pieces):
            m_old = m_ref[hh, :, c:c + tk]
            m_new = jnp.maximum(m_old, jnp.max(s, axis=0, keepdims=True))
            m_ref[hh, :, c:c + tk] = m_new
            weights.append((jnp.exp2(s - m_new).astype(BF16), jnp.exp2(m_old - m_new)))
        return weights

    def accumulate(j, pieces, weights):
        k0 = pl.multiple_of(j * tk, tk)
        for (hh, c), (pt, alpha) in zip(pieces, weights):
            vt = jnp.concatenate(
                [vt_ref[0, hh * HEAD_DIM:(hh + 1) * HEAD_DIM, pl.ds(k0, tk)], ones], axis=0)
            acc_ref[hh, :, c:c + tk] = alpha * acc_ref[hh, :, c:c + tk] + _dot(vt, pt)

    every = pieces_from(0)
    alpha_ref[...] = jnp.ones(alpha_ref.shape, F32)
    pt_ref[...] = jnp.zeros(pt_ref.shape, BF16)

    def put_scores(st):
        for s, (hh, c) in zip(st, every):
            st_ref[hh, :, c:c + tk] = s

    def get_scores():
        return [st_ref[hh, :, c:c + tk] for hh, c in every]

    def put_weights(weights):
        for (pt, alpha), (hh, c) in zip(weights, every):
            pt_ref[hh, :, c:c + tk] = pt
            alpha_ref[hh, :, c:c + tk] = alpha

    def get_weights():
        return [(pt_ref[hh, :, c:c + tk], alpha_ref[hh, :, c:c + tk]) for hh, c in every]

    def full_tile(j, carry):
        ahead = scores(j + 1, every)
        accumulate(jnp.maximum(j - 1, 0), every, get_weights())
        put_weights(softmax(get_scores(), every, False))
        put_scores(ahead)
        return carry

    put_scores(scores(first, every))
    lax.fori_loop(first, n_full, full_tile, 0)
    diagonal = [pieces_from(d * tk) for d in range(nb)]
    st = [get_scores()] + [scores(n_full + d, diagonal[d]) for d in range(1, nb)]
    accumulate(jnp.maximum(n_full - 1, 0), every, get_weights())
    for d in range(nb):
        accumulate(n_full + d, diagonal[d], softmax(st[d], diagonal[d], True))

    out = []
    for hh in (0, 1):
        acc = acc_ref[hh]
        out.append(_head_norm_t(acc[:HEAD_DIM] / acc[HEAD_DIM:HEAD_DIM + 1]))
    o_ref[0] = (jnp.concatenate(out, axis=0).T * g_ref[...]).astype(o_ref.dtype)


def _fox_attention(first, qt, k, vt, cc, gain, *, tq, tk):
    b, lp, width = k.shape
    pairs = width // LANES
    grid_spec = pltpu.PrefetchScalarGridSpec(
        num_scalar_prefetch=1,
        grid=(b, pairs, lp // tq),
        in_specs=[pl.BlockSpec((1, LANES, tq), lambda bi, p, qi, first: (bi, p, qi)),
                  pl.BlockSpec((1, lp, LANES), lambda bi, p, qi, first: (bi, 0, p)),
                  pl.BlockSpec((1, LANES, lp), lambda bi, p, qi, first: (bi, p, 0)),
                  pl.BlockSpec((1, lp, LANES), lambda bi, p, qi, first: (bi, 0, 0)),
                  pl.BlockSpec((1, LANES), lambda bi, p, qi, first: (0, p))],
        out_specs=pl.BlockSpec((1, tq, LANES), lambda bi, p, qi, first: (bi, qi, p)),
        scratch_shapes=[pltpu.VMEM((2, 2 * LANES, tq), BF16),
                        pltpu.VMEM((2, 1, tq), F32),
                        pltpu.VMEM((2, 1, tq), F32),
                        pltpu.VMEM((2, tk, tq), BF16),
                        pltpu.VMEM((2, tk, tq), F32),
                        pltpu.VMEM((2, HEAD_DIM + BF16_ROWS, tq), F32)])
    return pl.pallas_call(
        functools.partial(_fox_kernel, tq=tq, tk=tk),
        grid_spec=grid_spec,
        out_shape=jax.ShapeDtypeStruct((b, lp, width), BF16),
        compiler_params=pltpu.CompilerParams(
            dimension_semantics=("parallel", "parallel", "parallel"), vmem_limit_bytes=VMEM_LIMIT),
        name="fox_attention",
    )(first, qt, k, vt, cc, gain)


def _sb_kernel(qt_ref, k_ref, vt_ref, tri_ref, g_ref, o_ref, qa_ref, tail_ref, acc_ref,
               *, t, n_heads, group):
    gi = pl.program_id(1)
    pieces = [(a, h) for a in range(group) for h in range(n_heads)]
    for a, h in pieces:
        qt = qt_ref[0, h * HEAD_DIM:(h + 1) * HEAD_DIM, a * t:(a + 1) * t]
        zero = jnp.zeros_like(qt)
        qa_ref[a, h] = jnp.concatenate([qt, zero] if h % 2 == 0 else [zero, qt], axis=0)

    def key_tiles(back, masked):
        tri = tri_ref[...]
        own = [gi * group + a for a in range(group)]
        exists = [own[a] >= back for a in range(group)]
        k0 = [pl.multiple_of(jnp.maximum(own[a] - back, 0) * t, t) for a in range(group)]
        if masked:
            row = lax.broadcasted_iota(jnp.int32, (t, t), 0)
            col = lax.broadcasted_iota(jnp.int32, (t, t), 1)
            strict = row < col
        z, split, cum, weights, scale = {}, {}, {}, {}, {}

        def logits(a, h):
            z[a, h] = _dot(k_ref[0, pl.ds(k0[a], t), (h // 2) * LANES:(h // 2 + 1) * LANES], qa_ref[a, h])

        def softplus_parts(a, h):
            sp = jnp.maximum(z[a, h], 0.0) + jnp.log2(1.0 + jnp.exp2(-jnp.abs(z[a, h])))
            if masked:
                sp = jnp.where(strict, sp, 0.0)
            parts = [sp.astype(BF16)]
            for _ in range(SB_SUM_PARTS - 1):
                sp = sp - parts[-1].astype(F32)
                parts.append(sp.astype(BF16))
            split[a, h] = jnp.concatenate(parts, axis=0)

        def suffix_sums(a, h):
            cum[a, h] = _dot(tri, split.pop((a, h)))

        def stick_weights(a, h):
            c = cum.pop((a, h))
            e = z.pop((a, h)) - c
            if masked:
                e = jnp.where(strict, e, NEG_BIG)
            weights[a, h] = jnp.exp2(e).astype(BF16)
            if masked:
                tail_ref[a, h] = -c[:1, :]
            else:
                tail = jnp.where(exists[a], tail_ref[a, h], NEG_BIG)
                scale[a, h] = jnp.exp2(tail)
                tail_ref[a, h] = tail - c[:1, :]

        def accumulate(a, h):
            vt = vt_ref[0, h * HEAD_DIM:(h + 1) * HEAD_DIM, pl.ds(k0[a], t)]
            pv = _dot(vt, weights.pop((a, h)))
            acc_ref[a, h] = pv if masked else acc_ref[a, h] + scale.pop((a, h)) * pv

        stages = (logits, softplus_parts, suffix_sums, stick_weights, accumulate)
        wave = SB_WAVE_DIAGONAL if masked else SB_WAVE
        waves = [pieces[w:w + wave] for w in range(0, len(pieces), wave)]
        for step in range(len(waves) + len(stages) - 1):
            for s in reversed(range(len(stages))):
                if 0 <= step - s < len(waves):
                    for a, h in waves[step - s]:
                        stages[s](a, h)

    def live():
        return jnp.max(tail_ref[...]) > -ZERO_EXP2

    key_tiles(0, True)

    def cond(carry):
        back, go = carry
        return jnp.logical_and(back <= gi * group + (group - 1), go)

    def body(carry):
        back, _ = carry
        key_tiles(back, False)
        return back + 1, live()

    lax.while_loop(cond, body, (1, live()))
    for a in range(group):
        out = jnp.concatenate([_head_norm_t(acc_ref[a, h]) for h in range(n_heads)], axis=0)
        o_ref[0, a * t:(a + 1) * t, :] = (out.T * g_ref[...]).astype(o_ref.dtype)


def _sb_attention(qt, k, vt, gain, *, t, group):
    b, lp, width = k.shape
    n_heads = width // HEAD_DIM
    idx = jnp.arange(t)
    tri = (idx[None, :] >= idx[:, None]).astype(BF16)
    tri = jnp.concatenate([tri] * SB_SUM_PARTS, axis=1)
    return pl.pallas_call(
        functools.partial(_sb_kernel, t=t, n_heads=n_heads, group=group),
        grid=(b, lp // (group * t)),
        in_specs=[pl.BlockSpec((1, width, group * t), lambda bi, gi: (bi, 0, gi)),
                  pl.BlockSpec((1, lp, width), lambda bi, gi: (bi, 0, 0)),
                  pl.BlockSpec((1, width, lp), lambda bi, gi: (bi, 0, 0)),
                  pl.BlockSpec((t, SB_SUM_PARTS * t), lambda bi, gi: (0, 0)),
                  pl.BlockSpec((1, width), lambda bi, gi: (0, 0))],
        out_specs=pl.BlockSpec((1, group * t, width), lambda bi, gi: (bi, gi, 0)),
        out_shape=jax.ShapeDtypeStruct((b, lp, width), BF16),
        scratch_shapes=[pltpu.VMEM((group, n_heads, LANES, t), BF16),
                        pltpu.VMEM((group, n_heads, 1, t), F32),
                        pltpu.VMEM((group, n_heads, HEAD_DIM, t), F32)],
        compiler_params=pltpu.CompilerParams(
            dimension_semantics=("parallel", "parallel"), vmem_limit_bytes=VMEM_LIMIT),
        name="sb_attention",
    )(qt, k, vt, tri, gain)


def _out_proj_kernel(u_ref, up_ref, yf_ref, ys_ref, pw_ref, psc_ref, w_ref, o_ref, wo_ref, *, tm):
    i = pl.program_id(1)

    @pl.when((pl.program_id(0) == 0) & (i == 0))
    def _():
        for r0 in range(0, wo_ref.shape[0], WEIGHT_CAST_ROWS):
            wo_ref[r0:r0 + WEIGHT_CAST_ROWS, :] = w_ref[0, r0:r0 + WEIGHT_CAST_ROWS, :].astype(BF16)

    u = u_ref[0]
    pool_width = u.shape[1]
    o_ref[0] = _dot(jnp.concatenate([yf_ref[0], ys_ref[0]], axis=1), wo_ref[pool_width:, :])
    halo = jnp.where(i == 0, 0.0, up_ref[0])
    x = jnp.concatenate([halo, u], axis=0)
    sums = []
    shift = 1
    for _ in POOL_WINDOWS:
        x = x + pltpu.roll(x, shift, 0)
        sums.append(x[HALO:])
        shift *= 2
    group = lax.broadcasted_iota(jnp.int32, u.shape, 1) // POOL_GROUP
    window_sum = sums[-1]
    window = jnp.full(u.shape, POOL_WINDOWS[-1], jnp.int32)
    for g in range(len(POOL_WINDOWS) - 2, -1, -1):
        window_sum = jnp.where(group == g, sums[g], window_sum)
        window = jnp.where(group == g, POOL_WINDOWS[g], window)
    t1 = i * tm + lax.broadcasted_iota(jnp.int32, u.shape, 0) + 1
    count = jnp.minimum(t1, window).astype(F32)
    d = (window_sum / count - u).astype(BF16)
    y_pool = (_dot(d, pw_ref[...]) * psc_ref[...]).astype(BF16)
    o_ref[0] += _dot(y_pool, wo_ref[:pool_width, :])


def _out_proj(u, yf, ys, pw, psc, wo_all, layer, *, tm):
    b, lp, _ = u.shape
    d = wo_all.shape[2]
    assert wo_all.shape[1] % WEIGHT_CAST_ROWS == 0
    layer_weight = pl.BlockSpec((1,) + wo_all.shape[1:], lambda bi, i: (layer, 0, 0),
                                pipeline_mode=pl.Buffered(1))
    row3 = lambda w: pl.BlockSpec((1, tm, w), lambda bi, i: (bi, i, 0))
    full = lambda arr: pl.BlockSpec(arr.shape, lambda bi, i: (0,) * arr.ndim)
    halo_spec = pl.BlockSpec((1, HALO, u.shape[2]),
                             lambda bi, i: (bi, jnp.maximum(i * (tm // HALO) - 1, 0), 0))
    return pl.pallas_call(
        functools.partial(_out_proj_kernel, tm=tm),
        grid=(b, lp // tm),
        in_specs=[row3(u.shape[2]), halo_spec, row3(yf.shape[2]), row3(ys.shape[2]),
                  full(pw), full(psc), layer_weight],
        out_specs=row3(d),
        out_shape=jax.ShapeDtypeStruct((b, lp, d), F32),
        scratch_shapes=[pltpu.VMEM(wo_all.shape[1:], BF16)],
        compiler_params=pltpu.CompilerParams(
            dimension_semantics=("arbitrary", "arbitrary"), vmem_limit_bytes=VMEM_LIMIT),
        name="out_proj",
    )(u, u, yf, ys, pw, psc, wo_all)


def _ffn_kernel(h_ref, dh_ref, g_ref, wg_ref, wu_ref, wd_ref, fg_ref, o_ref, *, chunks, final_norm):
    h = h_ref[0] + dh_ref[0]
    a = _rms(h, g_ref[...]).astype(BF16)
    cw = wg_ref.shape[2] // chunks
    out = h
    for c in range(chunks):
        gate = _dot(a, wg_ref[0, :, c * cw:(c + 1) * cw])
        up = _dot(a, wu_ref[0, :, c * cw:(c + 1) * cw])
        act = (gate * jax.nn.sigmoid(gate) * up).astype(BF16)
        out = out + _dot(act, wd_ref[0, c * cw:(c + 1) * cw, :])
    if final_norm:
        out = _rms(out, fg_ref[...])
    o_ref[0] = out


def _ffn(h, dh, g, wg, wu, wd, layer, fg, *, tm, first_row, n_rows, final_norm):
    b, _, d = h.shape
    rows_in = pl.BlockSpec((pl.Element(1), pl.Element(tm), pl.Element(d)),
                           lambda bi, i: (bi, pl.multiple_of(first_row + i * tm, 8), 0))
    rows_out = pl.BlockSpec((1, tm, d), lambda bi, i: (bi, i, 0))
    resident = lambda arr: pl.BlockSpec(arr.shape, lambda bi, i: (0,) * arr.ndim,
                                        pipeline_mode=pl.Buffered(1))
    of_layer = lambda arr: pl.BlockSpec((1,) + arr.shape[1:], lambda bi, i: (layer, 0, 0),
                                        pipeline_mode=pl.Buffered(1))
    return pl.pallas_call(
        functools.partial(_ffn_kernel, chunks=FF_CHUNKS, final_norm=final_norm),
        grid=(b, n_rows // tm),
        in_specs=[rows_in, rows_in, resident(g), of_layer(wg), of_layer(wu), of_layer(wd), resident(fg)],
        out_specs=rows_out,
        out_shape=jax.ShapeDtypeStruct((b, n_rows, d), F32),
        compiler_params=pltpu.CompilerParams(
            dimension_semantics=("parallel", "parallel"), vmem_limit_bytes=VMEM_LIMIT),
        name="ffn",
    )(h, dh, g, wg, wu, wd, fg)


def _largest_tile(n, limit):
    return max(t for t in range(8, limit + 1, 8) if n % t == 0)


def _block_diag(pool_w):
    groups, cin, cout = pool_w.shape
    out = jnp.zeros((groups * cin, groups * cout), pool_w.dtype)
    for g in range(groups):
        out = out.at[g * cin:(g + 1) * cin, g * cout:(g + 1) * cout].set(pool_w[g])
    return out


def _trunk(x, meta_tokens, norm1, w_in, forget_bias, pool_w, pool_scale, fox_out_gain,
           sb_out_gain, w_out, norm2, w_gate, w_up, w_down, final_norm, *,
           attn_tile, fox_q_blocks, row_tile):
    b, s_len, d = x.shape
    depth = norm1.shape[0]
    l = N_META + s_len
    fox_tq = fox_q_blocks * attn_tile
    step = fox_tq * row_tile // math.gcd(fox_tq, row_tile)
    lp = -(-l // step) * step
    pool_width = pool_scale.shape[1]
    fox_w = fox_out_gain.shape[1]
    fox_heads = fox_w // HEAD_DIM
    assert fox_heads <= PART_STRIDE

    o0 = pool_width
    o1 = o0 + 3 * fox_w
    o2 = o1 + fox_heads
    for i in range(depth):
        fb = jnp.zeros((1, LANES), F32).at[0, :fox_heads].set(forget_bias[i].astype(F32))
        proj = dict(tm=row_tile, pool_width=pool_width, width=fox_w, n_heads=fox_heads)
        if i == 0:
            h, *outs = _in_proj(None, norm1[i][None], w_in.astype(F32), i, fb,
                                embed=(x.astype(F32), meta_tokens.astype(F32), lp), **proj)
        else:
            outs = _in_proj(h, norm1[i][None], w_in.astype(F32), i, fb, **proj)
        u, qft, kf, vft, cc, stats, qst, ks, vst = outs
        plan = _fox_plan(stats, tq=fox_tq, tk=attn_tile)
        first = plan[:, :lp // fox_tq, 0:fox_heads:2].transpose(0, 2, 1).reshape(-1)
        y_fox = _fox_attention(first, qft, kf, vft, cc, fox_out_gain[i][None], tq=fox_tq, tk=attn_tile)
        y_sb = _sb_attention(qst, ks, vst, sb_out_gain[i][None], t=attn_tile, group=fox_q_blocks)
        dh = _out_proj(u, y_fox, y_sb, _block_diag(pool_w[i]).astype(BF16), pool_scale[i][None],
                       w_out.astype(F32), i, tm=row_tile)
        last = i == depth - 1
        rows = dict(tm=_largest_tile(s_len, FFN_LAST_TILE), first_row=N_META, n_rows=s_len) if last \
            else dict(tm=row_tile, first_row=0, n_rows=lp)
        h = _ffn(h, dh, norm2[i][None], w_gate.astype(BF16), w_up.astype(BF16), w_down.astype(BF16), i,
                 final_norm[None], final_norm=last, **rows)
    return h


def kernel(x, meta_tokens, norm1, w_in, forget_bias, pool_w, pool_scale, fox_out_gain, sb_out_gain,
           w_out, norm2, w_gate, w_up, w_down, final_norm):
    return _trunk(x, meta_tokens, norm1, w_in, forget_bias, pool_w, pool_scale, fox_out_gain,
                  sb_out_gain, w_out, norm2, w_gate, w_up, w_down, final_norm,
                  attn_tile=ATTN_TILE, fox_q_blocks=FOX_Q_BLOCKS, row_tile=ROW_TILE)
```

```python
import functools
import math

import jax
import jax.numpy as jnp
from jax import lax
from jax.experimental import pallas as pl
from jax.experimental.pallas import tpu as pltpu

HEAD_DIM = 64
N_META = 16
EPS = 1e-6
POOL_WINDOWS = (2, 4, 8, 16)
POOL_GROUP = 64
LANES = 128
BF16_ROWS = 16
HALO = 16
PART_STRIDE = 8
LOG2E = math.log2(math.e)
NEG_BIG = -1e30
ZERO_EXP2 = 152.0
NORM_SLACK = 1.02
VMEM_LIMIT = 56 * 1024 * 1024
F32 = jnp.float32
BF16 = jnp.bfloat16

ATTN_TILE = 256
FOX_Q_BLOCKS = 3
ROW_TILE = 768
FFN_LAST_TILE = 1024
FF_CHUNKS = 11
WEIGHT_CAST_COLS = 384
WEIGHT_CAST_ROWS = 256
SB_WAVE = 1
SB_WAVE_DIAGONAL = 2
SB_SUM_PARTS = 1


def _dot(a, b):
    return jnp.dot(a, b, preferred_element_type=F32)


def _rms(x, gain):
    ms = jnp.mean(x * x, axis=-1, keepdims=True)
    return x * lax.rsqrt(ms + EPS) * gain


def _split3(x):
    hi = x.astype(BF16)
    r = x - hi.astype(F32)
    mid = r.astype(BF16)
    lo = (r - mid.astype(F32)).astype(BF16)
    return hi, mid, lo


def _pack_parts(x):
    hi, mid, lo = _split3(x)
    return (hi.astype(F32) + pltpu.roll(mid.astype(F32), PART_STRIDE, 1)
            + pltpu.roll(lo.astype(F32), 2 * PART_STRIDE, 1)).astype(BF16)


def _head_sq_norms(q, k):
    width = q.shape[1]
    sq = jnp.concatenate([(x.astype(F32) * x.astype(F32)).astype(BF16) for x in (q, k)], axis=1)
    d = lax.broadcasted_iota(jnp.int32, (2 * width, LANES), 0)
    lane = lax.broadcasted_iota(jnp.int32, (2 * width, LANES), 1)
    target = jnp.where(d < width, PART_STRIDE + d // HEAD_DIM, 2 * PART_STRIDE + (d - width) // HEAD_DIM)
    return _dot(sq, (lane == target).astype(BF16))


def _in_proj_kernel(h_ref, g_ref, w_ref, fb_ref, *outs_and_scratch, **static):
    _in_proj_tile(h_ref[0], g_ref, w_ref, fb_ref, *outs_and_scratch, **static)


def _embed_in_proj_kernel(x_ref, x_halo_ref, x_tail_ref, meta_ref, g_ref, w_ref, fb_ref, h_ref,
                          *outs_and_scratch, n_full, **static):
    i = pl.program_id(1)
    rows = h_ref.shape[1] - N_META
    tail = jnp.concatenate([x_tail_ref[0], jnp.zeros((rows - x_tail_ref.shape[1], h_ref.shape[2]), F32)],
                           axis=0)
    body = jnp.where(i < n_full, x_ref[0, :rows], jnp.where(i == n_full, tail, 0.0))
    head = jnp.where(i == 0, meta_ref[...], jnp.where(i <= n_full, x_halo_ref[0], 0.0))
    tile = jnp.concatenate([head, body], axis=0)
    h_ref[0] = tile
    _in_proj_tile(tile, g_ref, w_ref, fb_ref, *outs_and_scratch, **static)


def _in_proj_tile(h, g_ref, w_ref, fb_ref,
                  u_ref, qf_ref, kf_ref, vf_ref, cc_ref, plan_ref, qs_ref, ks_ref, vs_ref,
                  wb_ref, carry_ref, stats_ref, *, tm, width, n_heads, tq, tk):
    i = pl.program_id(1)
    pool_width = u_ref.shape[2]
    fox_cols = 3 * width + LANES

    @pl.when((pl.program_id(0) == 0) & (i == 0))
    def _():
        aligned = pool_width + 3 * width

        def copy(dst, src, cols):
            for c0 in range(0, cols, WEIGHT_CAST_COLS):
                n = min(WEIGHT_CAST_COLS, cols - c0)
                wb_ref[:, dst + c0:dst + c0 + n] = w_ref[0, :, src + c0:src + c0 + n].astype(BF16)

        copy(0, 0, aligned)
        forget = w_ref[0, :, aligned:aligned + LANES]
        lane = lax.broadcasted_iota(jnp.int32, forget.shape, 1)
        wb_ref[:, aligned:aligned + LANES] = jnp.where(lane < n_heads, forget, 0.0).astype(BF16)
        copy(aligned + LANES, aligned + n_heads, 3 * width)

    @pl.when(i == 0)
    def _():
        carry_ref[...] = jnp.zeros_like(carry_ref)

    a = _rms(h, g_ref[...]).astype(BF16)
    u_ref[0] = _dot(a, wb_ref[:, :pool_width])

    scale = HEAD_DIM ** -0.5
    pf = _dot(a, wb_ref[:, pool_width:pool_width + fox_cols])
    qf = pf[:, :width] * (scale * LOG2E)
    kf = pf[:, width:2 * width].astype(BF16)
    qf_ref[0] = qf.T.astype(BF16)
    kf_ref[0] = kf
    vf_ref[0] = pf[:, 2 * width:3 * width].T.astype(BF16)
    ps = _dot(a, wb_ref[:, pool_width + fox_cols:])
    qs_ref[0] = (ps[:, :width] * (scale * LOG2E)).T.astype(BF16)
    ks_ref[0] = ps[:, width:2 * width].astype(BF16)
    vs_ref[0] = ps[:, 2 * width:].T.astype(BF16)

    fl = pf[:, 3 * width:] + fb_ref[...]
    lane = lax.broadcasted_iota(jnp.int32, fl.shape, 1)
    log_f = jnp.where(lane < n_heads,
                      (jnp.minimum(fl, 0.0) - jnp.log1p(jnp.exp(-jnp.abs(fl)))) * LOG2E, 0.0)
    row = lax.broadcasted_iota(jnp.int32, (tm, tm), 0)
    col = lax.broadcasted_iota(jnp.int32, (tm, tm), 1)
    sums = _dot((col <= row).astype(BF16), _pack_parts(log_f))
    sums = sums + pltpu.roll(sums, LANES - PART_STRIDE, 1) + pltpu.roll(sums, LANES - 2 * PART_STRIDE, 1)
    c = jnp.where(lane < PART_STRIDE, sums, 0.0) + carry_ref[:1, :]
    carry_ref[...] = jnp.broadcast_to(c[tm - 1:tm, :], carry_ref.shape)
    cc_ref[0] = _pack_parts(c)
    stats_ref[pl.ds(pl.multiple_of(i * tm, tm), tm), :] = c + _head_sq_norms(qf.astype(BF16), kf)

    @pl.when(i == pl.num_programs(1) - 1)
    def _():
        plan_ref[0] = _fox_first_tiles(stats_ref, tq=tq, tk=tk, rows_out=plan_ref.shape[1])


def _in_proj(h, g, w_all, layer, fb, *, tm, pool_width, width, n_heads, tq, tk, embed=None):
    if embed is None:
        b, lp, d = h.shape
    else:
        x, meta, lp = embed
        b, s_len, d = x.shape
    bf16_cols = pool_width + 3 * width + LANES + 3 * width
    row3 = lambda w: pl.BlockSpec((1, tm, w), lambda bi, i: (bi, i, 0))
    col3 = lambda w: pl.BlockSpec((1, w, tm), lambda bi, i: (bi, 0, i))
    full = lambda arr: pl.BlockSpec(arr.shape, lambda bi, i: (0,) * arr.ndim)
    layer_weight = pl.BlockSpec((1,) + w_all.shape[1:], lambda bi, i: (layer, 0, 0),
                                pipeline_mode=pl.Buffered(1))
    rows = jax.ShapeDtypeStruct((b, lp, width), BF16)
    cols = jax.ShapeDtypeStruct((b, width, lp), BF16)
    static = dict(tm=tm, width=width, n_heads=n_heads, tq=tq, tk=tk)
    plan_rows = -(-(lp // tq) // 8) * 8
    in_specs = [full(g), layer_weight, full(fb)]
    out_specs = [row3(pool_width), col3(width), row3(width), col3(width), row3(LANES),
                 pl.BlockSpec((1, plan_rows, LANES), lambda bi, i: (bi, 0, 0)),
                 col3(width), row3(width), col3(width)]
    out_shape = [jax.ShapeDtypeStruct((b, lp, pool_width), F32), cols, rows, cols,
                 jax.ShapeDtypeStruct((b, lp, LANES), BF16),
                 jax.ShapeDtypeStruct((b, plan_rows, LANES), jnp.int32), cols, rows, cols]
    if embed is None:
        body, operands = functools.partial(_in_proj_kernel, **static), (h,)
        in_specs = [row3(d)] + in_specs
    else:
        n_full = s_len // tm
        rem = s_len - n_full * tm
        assert N_META == HALO and rem % 8 == 0 and 0 < rem <= tm - N_META and s_len % N_META == 0
        last_halo = s_len // N_META - 1
        x_specs = [
            pl.BlockSpec((1, tm, d), lambda bi, i: (bi, jnp.minimum(i, n_full - 1), 0)),
            pl.BlockSpec((1, N_META, d),
                         lambda bi, i: (bi, jnp.clip(i * (tm // N_META) - 1, 0, last_halo), 0)),
            pl.BlockSpec((pl.Element(1), pl.Element(rem), pl.Element(d)),
                         lambda bi, i: (bi, n_full * tm, 0)),
        ]
        body = functools.partial(_embed_in_proj_kernel, n_full=n_full, **static)
        operands = (x, x, x, meta)
        in_specs = x_specs + [full(meta)] + in_specs
        out_specs = [row3(d)] + out_specs
        out_shape = [jax.ShapeDtypeStruct((b, lp, d), F32)] + out_shape
    return pl.pallas_call(
        body,
        grid=(b, lp // tm),
        in_specs=in_specs,
        out_specs=out_specs,
        out_shape=out_shape,
        scratch_shapes=[pltpu.VMEM((d, bf16_cols), BF16), pltpu.VMEM((8, LANES), F32),
                        pltpu.VMEM((lp, LANES), F32)],
        compiler_params=pltpu.CompilerParams(
            dimension_semantics=("arbitrary", "arbitrary"), vmem_limit_bytes=VMEM_LIMIT),
        name="in_proj",
    )(*operands, g, w_all, fb)


def _fox_first_tiles(stats_ref, *, tq, tk, rows_out):
    s = stats_ref[...]
    lp = s.shape[0]
    lane = lax.broadcasted_iota(jnp.int32, s.shape, 1)
    k_max_sq = pltpu.roll(jnp.max(s, axis=0, keepdims=True), LANES - 2 * PART_STRIDE, 1)
    q_sq = pltpu.roll(s, LANES - PART_STRIDE, 1)
    spread = jnp.where(lane < PART_STRIDE, jnp.sqrt(q_sq * k_max_sq), 0.0)
    reach = s + (2.0 * NORM_SLACK) * spread + ZERO_EXP2
    tiles = [jnp.max(reach[i * tq:(i + 1) * tq], axis=0, keepdims=True) for i in range(lp // tq)]
    tiles.append(jnp.full((rows_out - len(tiles), LANES), -NEG_BIG, F32))
    reach_max = jnp.concatenate(tiles, axis=0)
    n_key_tiles = lp // tk
    c_end = stats_ref[pl.ds(tk - 1, n_key_tiles, stride=tk), :]
    count = jnp.zeros(reach_max.shape, jnp.int32)
    for j in range(n_key_tiles):
        count = count + (c_end[j:j + 1, :] > reach_max).astype(jnp.int32)
    return jnp.minimum(count, pltpu.roll(count, LANES - 1, 1))


def _head_norm_t(o_t):
    ms = jnp.mean(o_t * o_t, axis=0, keepdims=True)
    return o_t * lax.rsqrt(ms + EPS)


def _fox_kernel(first_ref, qt_ref, k_ref, vt_ref, cc_ref, g_ref, o_ref,
                qa_ref, m_ref, alpha_ref, pt_ref, st_ref, acc_ref, *, tq, tk):
    p = pl.program_id(1)
    qi = pl.program_id(2)
    nb = tq // tk
    n_full = nb * qi
    step = (pl.program_id(0) * pl.num_programs(1) + p) * pl.num_programs(2) + qi
    first = jnp.minimum(first_ref[step], n_full)

    qt = qt_ref[0]
    r = lax.broadcasted_iota(jnp.int32, qt.shape, 0)
    zero = jnp.zeros_like(qt)
    for hh in (0, 1):
        own = (r < HEAD_DIM) if hh == 0 else (r >= HEAD_DIM)
        h = 2 * p + hh
        picks = (r == h) | (r == h + PART_STRIDE) | (r == h + 2 * PART_STRIDE)
        qa_ref[hh, :LANES, :] = jnp.where(own, qt, zero)
        qa_ref[hh, LANES:, :] = jnp.where(picks, -1.0, 0.0).astype(BF16)
    m_ref[...] = jnp.full(m_ref.shape, NEG_BIG, F32)
    acc_ref[...] = jnp.zeros(acc_ref.shape, F32)
    ones = jnp.ones((BF16_ROWS, tk), BF16)

    def pieces_from(c0):
        return [(hh, c) for hh in (0, 1) for c in range(c0, tq, tk)]

    def scores(j, pieces):
        k0 = pl.multiple_of(j * tk, tk)
        kk = jnp.concatenate([k_ref[0, pl.ds(k0, tk), :], cc_ref[0, pl.ds(k0, tk), :]], axis=1)
        return [_dot(kk, qa_ref[hh, :, c:c + tk]) for hh, c in pieces]

    def softmax(st, pieces, masked):
        if masked:
            row = lax.broadcasted_iota(jnp.int32, (tk, tk), 0)
            col = lax.broadcasted_iota(jnp.int32, (tk, tk), 1)
            c0 = pieces[0][1]
            st = [jnp.where(row <= col, s, NEG_BIG) if c == c0 else s for s, (_, c) in zip(st, pieces)]
        weights = []
        for s, (hh, c) in zip(st, pieces):
            m_old = m_ref[hh, :, c:c + tk]
            m_new = jnp.maximum(m_old, jnp.max(s, axis=0, keepdims=True))
            m_ref[hh, :, c:c + tk] = m_new
            weights.append((jnp.exp2(s - m_new).astype(BF16), jnp.exp2(m_old - m_new)))
        return weights

    def accumulate(j, pieces, weights):
        k0 = pl.multiple_of(j * tk, tk)
        for (hh, c), (pt, alpha) in zip(pieces, weights):
            vt = jnp.concatenate(
                [vt_ref[0, hh * HEAD_DIM:(hh + 1) * HEAD_DIM, pl.ds(k0, tk)], ones], axis=0)
            acc_ref[hh, :, c:c + tk] = alpha * acc_ref[hh, :, c:c + tk] + _dot(vt, pt)

    every = pieces_from(0)
    alpha_ref[...] = jnp.ones(alpha_ref.shape, F32)
    pt_ref[...] = jnp.zeros(pt_ref.shape, BF16)

    def put_scores(st):
        for s, (hh, c) in zip(st, every):
            st_ref[hh, :, c:c + tk] = s

    def get_scores():
        return [st_ref[hh, :, c:c + tk] for hh, c in every]

    def put_weights(weights):
        for (pt, alpha), (hh, c) in zip(weights, every):
            pt_ref[hh, :, c:c + tk] = pt
            alpha_ref[hh, :, c:c + tk] = alpha

    def get_weights():
        return [(pt_ref[hh, :, c:c + tk], alpha_ref[hh, :, c:c + tk]) for hh, c in every]

    def full_tile(j, carry):
        ahead = scores(j + 1, every)
        accumulate(jnp.maximum(j - 1, 0), every, get_weights())
        put_weights(softmax(get_scores(), every, False))
        put_scores(ahead)
        return carry

    put_scores(scores(first, every))
    lax.fori_loop(first, n_full, full_tile, 0)
    diagonal = [pieces_from(d * tk) for d in range(nb)]
    st = [get_scores()] + [scores(n_full + d, diagonal[d]) for d in range(1, nb)]
    accumulate(jnp.maximum(n_full - 1, 0), every, get_weights())
    for d in range(nb):
        accumulate(n_full + d, diagonal[d], softmax(st[d], diagonal[d], True))

    out = []
    for hh in (0, 1):
        acc = acc_ref[hh]
        out.append(_head_norm_t(acc[:HEAD_DIM] / acc[HEAD_DIM:HEAD_DIM + 1]))
    o_ref[0] = (jnp.concatenate(out, axis=0).T * g_ref[...]).astype(o_ref.dtype)


def _fox_attention(first, qt, k, vt, cc, gain, *, tq, tk):
    b, lp, width = k.shape
    pairs = width // LANES
    grid_spec = pltpu.PrefetchScalarGridSpec(
        num_scalar_prefetch=1,
        grid=(b, pairs, lp // tq),
        in_specs=[pl.BlockSpec((1, LANES, tq), lambda bi, p, qi, first: (bi, p, qi)),
                  pl.BlockSpec((1, lp, LANES), lambda bi, p, qi, first: (bi, 0, p)),
                  pl.BlockSpec((1, LANES, lp), lambda bi, p, qi, first: (bi, p, 0)),
                  pl.BlockSpec((1, lp, LANES), lambda bi, p, qi, first: (bi, 0, 0)),
                  pl.BlockSpec((1, LANES), lambda bi, p, qi, first: (0, p))],
        out_specs=pl.BlockSpec((1, tq, LANES), lambda bi, p, qi, first: (bi, qi, p)),
        scratch_shapes=[pltpu.VMEM((2, 2 * LANES, tq), BF16),
                        pltpu.VMEM((2, 1, tq), F32),
                        pltpu.VMEM((2, 1, tq), F32),
                        pltpu.VMEM((2, tk, tq), BF16),
                        pltpu.VMEM((2, tk, tq), F32),
                        pltpu.VMEM((2, HEAD_DIM + BF16_ROWS, tq), F32)])
    return pl.pallas_call(
        functools.partial(_fox_kernel, tq=tq, tk=tk),
        grid_spec=grid_spec,
        out_shape=jax.ShapeDtypeStruct((b, lp, width), BF16),
        compiler_params=pltpu.CompilerParams(
            dimension_semantics=("parallel", "parallel", "parallel"), vmem_limit_bytes=VMEM_LIMIT),
        name="fox_attention",
    )(first, qt, k, vt, cc, gain)


def _sb_kernel(qt_ref, k_ref, vt_ref, tri_ref, g_ref, o_ref, qa_ref, tail_ref, acc_ref,
               *, t, n_heads, group):
    gi = pl.program_id(1)
    pieces = [(a, h) for a in range(group) for h in range(n_heads)]
    for a, h in pieces:
        qt = qt_ref[0, h * HEAD_DIM:(h + 1) * HEAD_DIM, a * t:(a + 1) * t]
        zero = jnp.zeros_like(qt)
        qa_ref[a, h] = jnp.concatenate([qt, zero] if h % 2 == 0 else [zero, qt], axis=0)

    def key_tiles(back, masked):
        tri = tri_ref[...]
        own = [gi * group + a for a in range(group)]
        exists = [own[a] >= back for a in range(group)]
        k0 = [pl.multiple_of(jnp.maximum(own[a] - back, 0) * t, t) for a in range(group)]
        if masked:
            row = lax.broadcasted_iota(jnp.int32, (t, t), 0)
            col = lax.broadcasted_iota(jnp.int32, (t, t), 1)
            strict = row < col
        z, split, cum, weights, scale = {}, {}, {}, {}, {}

        def logits(a, h):
            z[a, h] = _dot(k_ref[0, pl.ds(k0[a], t), (h // 2) * LANES:(h // 2 + 1) * LANES], qa_ref[a, h])

        def softplus_parts(a, h):
            sp = jnp.maximum(z[a, h], 0.0) + jnp.log2(1.0 + jnp.exp2(-jnp.abs(z[a, h])))
            if masked:
                sp = jnp.where(strict, sp, 0.0)
            parts = [sp.astype(BF16)]
            for _ in range(SB_SUM_PARTS - 1):
                sp = sp - parts[-1].astype(F32)
                parts.append(sp.astype(BF16))
            split[a, h] = jnp.concatenate(parts, axis=0)

        def suffix_sums(a, h):
            cum[a, h] = _dot(tri, split.pop((a, h)))

        def stick_weights(a, h):
            c = cum.pop((a, h))
            e = z.pop((a, h)) - c
            if masked:
                e = jnp.where(strict, e, NEG_BIG)
            weights[a, h] = jnp.exp2(e).astype(BF16)
            if masked:
                tail_ref[a, h] = -c[:1, :]
            else:
                tail = jnp.where(exists[a], tail_ref[a, h], NEG_BIG)
                scale[a, h] = jnp.exp2(tail)
                tail_ref[a, h] = tail - c[:1, :]

        def accumulate(a, h):
            vt = vt_ref[0, h * HEAD_DIM:(h + 1) * HEAD_DIM, pl.ds(k0[a], t)]
            pv = _dot(vt, weights.pop((a, h)))
            acc_ref[a, h] = pv if masked else acc_ref[a, h] + scale.pop((a, h)) * pv

        stages = (logits, softplus_parts, suffix_sums, stick_weights, accumulate)
        wave = SB_WAVE_DIAGONAL if masked else SB_WAVE
        waves = [pieces[w:w + wave] for w in range(0, len(pieces), wave)]
        for step in range(len(waves) + len(stages) - 1):
            for s in reversed(range(len(stages))):
                if 0 <= step - s < len(waves):
                    for a, h in waves[step - s]:
                        stages[s](a, h)

    def live():
        return jnp.max(tail_ref[...]) > -ZERO_EXP2

    key_tiles(0, True)

    def cond(carry):
        back, go = carry
        return jnp.logical_and(back <= gi * group + (group - 1), go)

    def body(carry):
        back, _ = carry
        key_tiles(back, False)
        return back + 1, live()

    lax.while_loop(cond, body, (1, live()))
    for a in range(group):
        out = jnp.concatenate([_head_norm_t(acc_ref[a, h]) for h in range(n_heads)], axis=0)
        o_ref[0, a * t:(a + 1) * t, :] = (out.T * g_ref[...]).astype(o_ref.dtype)


def _sb_attention(qt, k, vt, gain, *, t, group):
    b, lp, width = k.shape
    n_heads = width // HEAD_DIM
    idx = jnp.arange(t)
    tri = (idx[None, :] >= idx[:, None]).astype(BF16)
    tri = jnp.concatenate([tri] * SB_SUM_PARTS, axis=1)
    return pl.pallas_call(
        functools.partial(_sb_kernel, t=t, n_heads=n_heads, group=group),
        grid=(b, lp // (group * t)),
        in_specs=[pl.BlockSpec((1, width, group * t), lambda bi, gi: (bi, 0, gi)),
                  pl.BlockSpec((1, lp, width), lambda bi, gi: (bi, 0, 0)),
                  pl.BlockSpec((1, width, lp), lambda bi, gi: (bi, 0, 0)),
                  pl.BlockSpec((t, SB_SUM_PARTS * t), lambda bi, gi: (0, 0)),
                  pl.BlockSpec((1, width), lambda bi, gi: (0, 0))],
        out_specs=pl.BlockSpec((1, group * t, width), lambda bi, gi: (bi, gi, 0)),
        out_shape=jax.ShapeDtypeStruct((b, lp, width), BF16),
        scratch_shapes=[pltpu.VMEM((group, n_heads, LANES, t), BF16),
                        pltpu.VMEM((group, n_heads, 1, t), F32),
                        pltpu.VMEM((group, n_heads, HEAD_DIM, t), F32)],
        compiler_params=pltpu.CompilerParams(
            dimension_semantics=("parallel", "parallel"), vmem_limit_bytes=VMEM_LIMIT),
        name="sb_attention",
    )(qt, k, vt, tri, gain)


def _out_proj_kernel(u_ref, up_ref, yf_ref, ys_ref, pw_ref, psc_ref, w_ref, o_ref, wo_ref, *, tm):
    i = pl.program_id(1)

    @pl.when((pl.program_id(0) == 0) & (i == 0))
    def _():
        for r0 in range(0, wo_ref.shape[0], WEIGHT_CAST_ROWS):
            wo_ref[r0:r0 + WEIGHT_CAST_ROWS, :] = w_ref[0, r0:r0 + WEIGHT_CAST_ROWS, :].astype(BF16)

    u = u_ref[0]
    pool_width = u.shape[1]
    o_ref[0] = _dot(jnp.concatenate([yf_ref[0], ys_ref[0]], axis=1), wo_ref[pool_width:, :])
    halo = jnp.where(i == 0, 0.0, up_ref[0])
    x = jnp.concatenate([halo, u], axis=0)
    sums = []
    shift = 1
    for _ in POOL_WINDOWS:
        x = x + pltpu.roll(x, shift, 0)
        sums.append(x[HALO:])
        shift *= 2
    group = lax.broadcasted_iota(jnp.int32, u.shape, 1) // POOL_GROUP
    window_sum = sums[-1]
    window = jnp.full(u.shape, POOL_WINDOWS[-1], jnp.int32)
    for g in range(len(POOL_WINDOWS) - 2, -1, -1):
        window_sum = jnp.where(group == g, sums[g], window_sum)
        window = jnp.where(group == g, POOL_WINDOWS[g], window)
    t1 = i * tm + lax.broadcasted_iota(jnp.int32, u.shape, 0) + 1
    count = jnp.minimum(t1, window).astype(F32)
    d = (window_sum / count - u).astype(BF16)
    y_pool = (_dot(d, pw_ref[...]) * psc_ref[...]).astype(BF16)
    o_ref[0] += _dot(y_pool, wo_ref[:pool_width, :])


def _out_proj(u, yf, ys, pw, psc, wo_all, layer, *, tm):
    b, lp, _ = u.shape
    d = wo_all.shape[2]
    assert wo_all.shape[1] % WEIGHT_CAST_ROWS == 0
    layer_weight = pl.BlockSpec((1,) + wo_all.shape[1:], lambda bi, i: (layer, 0, 0),
                                pipeline_mode=pl.Buffered(1))
    row3 = lambda w: pl.BlockSpec((1, tm, w), lambda bi, i: (bi, i, 0))
    full = lambda arr: pl.BlockSpec(arr.shape, lambda bi, i: (0,) * arr.ndim)
    halo_spec = pl.BlockSpec((1, HALO, u.shape[2]),
                             lambda bi, i: (bi, jnp.maximum(i * (tm // HALO) - 1, 0), 0))
    return pl.pallas_call(
        functools.partial(_out_proj_kernel, tm=tm),
        grid=(b, lp // tm),
        in_specs=[row3(u.shape[2]), halo_spec, row3(yf.shape[2]), row3(ys.shape[2]),
                  full(pw), full(psc), layer_weight],
        out_specs=row3(d),
        out_shape=jax.ShapeDtypeStruct((b, lp, d), F32),
        scratch_shapes=[pltpu.VMEM(wo_all.shape[1:], BF16)],
        compiler_params=pltpu.CompilerParams(
            dimension_semantics=("arbitrary", "arbitrary"), vmem_limit_bytes=VMEM_LIMIT),
        name="out_proj",
    )(u, u, yf, ys, pw, psc, wo_all)


def _ffn_kernel(h_ref, dh_ref, g_ref, wg_ref, wu_ref, wd_ref, fg_ref, o_ref, *, chunks, final_norm):
    h = h_ref[0] + dh_ref[0]
    a = _rms(h, g_ref[...]).astype(BF16)
    cw = wg_ref.shape[2] // chunks
    out = h
    for c in range(chunks):
        gate = _dot(a, wg_ref[0, :, c * cw:(c + 1) * cw])
        up = _dot(a, wu_ref[0, :, c * cw:(c + 1) * cw])
        act = (gate * jax.nn.sigmoid(gate) * up).astype(BF16)
        out = out + _dot(act, wd_ref[0, c * cw:(c + 1) * cw, :])
    if final_norm:
        out = _rms(out, fg_ref[...])
    o_ref[0] = out


def _ffn(h, dh, g, wg, wu, wd, layer, fg, *, tm, first_row, n_rows, final_norm):
    b, _, d = h.shape
    rows_in = pl.BlockSpec((pl.Element(1), pl.Element(tm), pl.Element(d)),
                           lambda bi, i: (bi, pl.multiple_of(first_row + i * tm, 8), 0))
    rows_out = pl.BlockSpec((1, tm, d), lambda bi, i: (bi, i, 0))
    resident = lambda arr: pl.BlockSpec(arr.shape, lambda bi, i: (0,) * arr.ndim,
                                        pipeline_mode=pl.Buffered(1))
    of_layer = lambda arr: pl.BlockSpec((1,) + arr.shape[1:], lambda bi, i: (layer, 0, 0),
                                        pipeline_mode=pl.Buffered(1))
    return pl.pallas_call(
        functools.partial(_ffn_kernel, chunks=FF_CHUNKS, final_norm=final_norm),
        grid=(b, n_rows // tm),
        in_specs=[rows_in, rows_in, resident(g), of_layer(wg), of_layer(wu), of_layer(wd), resident(fg)],
        out_specs=rows_out,
        out_shape=jax.ShapeDtypeStruct((b, n_rows, d), F32),
        compiler_params=pltpu.CompilerParams(
            dimension_semantics=("parallel", "parallel"), vmem_limit_bytes=VMEM_LIMIT),
        name="ffn",
    )(h, dh, g, wg, wu, wd, fg)


def _largest_tile(n, limit):
    return max(t for t in range(8, limit + 1, 8) if n % t == 0)


def _block_diag(pool_w):
    groups, cin, cout = pool_w.shape
    out = jnp.zeros((groups * cin, groups * cout), pool_w.dtype)
    for g in range(groups):
        out = out.at[g * cin:(g + 1) * cin, g * cout:(g + 1) * cout].set(pool_w[g])
    return out


def _trunk(x, meta_tokens, norm1, w_in, forget_bias, pool_w, pool_scale, fox_out_gain,
           sb_out_gain, w_out, norm2, w_gate, w_up, w_down, final_norm, *,
           attn_tile, fox_q_blocks, row_tile):
    b, s_len, d = x.shape
    depth = norm1.shape[0]
    l = N_META + s_len
    fox_tq = fox_q_blocks * attn_tile
    step = fox_tq * row_tile // math.gcd(fox_tq, row_tile)
    lp = -(-l // step) * step
    pool_width = pool_scale.shape[1]
    fox_w = fox_out_gain.shape[1]
    fox_heads = fox_w // HEAD_DIM
    assert fox_heads <= PART_STRIDE

    o0 = pool_width
    o1 = o0 + 3 * fox_w
    o2 = o1 + fox_heads
    for i in range(depth):
        fb = jnp.zeros((1, LANES), F32).at[0, :fox_heads].set(forget_bias[i].astype(F32))
        proj = dict(tm=row_tile, pool_width=pool_width, width=fox_w, n_heads=fox_heads,
                    tq=fox_tq, tk=attn_tile)
        if i == 0:
            h, *outs = _in_proj(None, norm1[i][None], w_in.astype(F32), i, fb,
                                embed=(x.astype(F32), meta_tokens.astype(F32), lp), **proj)
        else:
            outs = _in_proj(h, norm1[i][None], w_in.astype(F32), i, fb, **proj)
        u, qft, kf, vft, cc, plan, qst, ks, vst = outs
        first = plan[:, :lp // fox_tq, 0:fox_heads:2].transpose(0, 2, 1).reshape(-1)
        y_fox = _fox_attention(first, qft, kf, vft, cc, fox_out_gain[i][None], tq=fox_tq, tk=attn_tile)
        y_sb = _sb_attention(qst, ks, vst, sb_out_gain[i][None], t=attn_tile, group=fox_q_blocks)
        dh = _out_proj(u, y_fox, y_sb, _block_diag(pool_w[i]).astype(BF16), pool_scale[i][None],
                       w_out.astype(F32), i, tm=row_tile)
        last = i == depth - 1
        rows = dict(tm=_largest_tile(s_len, FFN_LAST_TILE), first_row=N_META, n_rows=s_len) if last \
            else dict(tm=row_tile, first_row=0, n_rows=lp)
        h = _ffn(h, dh, norm2[i][None], w_gate.astype(BF16), w_up.astype(BF16), w_down.astype(BF16), i,
                 final_norm[None], final_norm=last, **rows)
    return h


def kernel(x, meta_tokens, norm1, w_in, forget_bias, pool_w, pool_scale, fox_out_gain, sb_out_gain,
           w_out, norm2, w_gate, w_up, w_down, final_norm):
    return _trunk(x, meta_tokens, norm1, w_in, forget_bias, pool_w, pool_scale, fox_out_gain,
                  sb_out_gain, w_out, norm2, w_gate, w_up, w_down, final_norm,
                  attn_tile=ATTN_TILE, fox_q_blocks=FOX_Q_BLOCKS, row_tile=ROW_TILE)
```
